```python
import jax, jax.numpy as jnp
from jax import lax
import numpy as np

D_MODEL = 1024
BATCH = 4
SEQ = 4096
DEPTH = 4

EXPAND = 2
D_INNER = EXPAND * D_MODEL
N_HEADS = 8
HEAD_QK = D_MODEL // N_HEADS
HEAD_V = D_INNER // N_HEADS
QK_W = N_HEADS * HEAD_QK
CONV_K = 4
CHUNK = 64
NORM_EPS = 1e-6
N_MIXERS = 2
N_GDN = (DEPTH + 1) // 2
N_MLSTM = DEPTH // 2
GDN_IN = 2 * QK_W + 2 * D_INNER + 2 * N_HEADS
MLSTM_IN = 2 * QK_W + 3 * D_INNER + 2 * N_HEADS

kernel_name = 'hybrid_gdn_mlstm_interleaved'


def rmsnorm(x, w):
    xf = x.astype(jnp.float32)
    y = xf * lax.rsqrt(jnp.mean(xf * xf, -1, keepdims=True) + NORM_EPS)
    return (y * w.astype(jnp.float32)).astype(x.dtype)


def l2norm(t):
    return t * lax.rsqrt(jnp.sum(t * t, -1, keepdims=True) + 1e-6)


def causal_dwconv(x, w):
    return lax.conv_general_dilated(x, w[:, None, :].astype(x.dtype), window_strides=(1,),
                                    padding=[(CONV_K - 1, 0)],
                                    dimension_numbers=('NWC', 'WIO', 'NWC'),
                                    feature_group_count=x.shape[-1])


def to_chunks(t):
    b, s, h = t.shape[:3]
    t = t.reshape((b, s // CHUNK, CHUNK, h) + t.shape[3:])
    return jnp.moveaxis(t, (1, 3), (0, 2))


def from_chunks(t):
    n, b, h, c, d = t.shape
    return jnp.moveaxis(t, (0, 2), (1, 3)).reshape(b, n * c, h, d)


def gated_delta_chunked(q, k, v, g, beta):
    q, k, v, g, beta = (to_chunks(t) for t in (q, k, v, g, beta))
    incl = jnp.tril(jnp.ones((CHUNK, CHUNK), bool))
    strict = jnp.tril(jnp.ones((CHUNK, CHUNK), bool), -1)
    gc = jnp.cumsum(g, -1)
    diff = gc[..., :, None] - gc[..., None, :]
    L = jnp.where(incl, jnp.exp(jnp.where(incl, diff, 0.0)), 0.0)
    kk = jnp.einsum('nbhid,nbhjd->nbhij', k, k)
    a = jnp.eye(CHUNK, dtype=jnp.float32) + jnp.where(strict, beta[..., :, None] * kk * L, 0.0)
    u = lax.linalg.triangular_solve(a, beta[..., None] * v, left_side=True, lower=True, unit_diagonal=True)
    wk = lax.linalg.triangular_solve(a, (beta * jnp.exp(gc))[..., None] * k, left_side=True, lower=True,
                                     unit_diagonal=True)
    attn = jnp.einsum('nbhid,nbhjd->nbhij', q, k) * L
    qg = q * jnp.exp(gc)[..., None]
    kd = k * jnp.exp(gc[..., -1:] - gc)[..., None]
    decay = jnp.exp(gc[..., -1])

    def step(S, xs):
        u_c, wk_c, attn_c, qg_c, kd_c, dec_c = xs
        w = u_c - jnp.einsum('bhcd,bhde->bhce', wk_c, S)
        o = jnp.einsum('bhcd,bhde->bhce', qg_c, S) + jnp.einsum('bhij,bhje->bhie', attn_c, w)
        S = dec_c[..., None, None] * S + jnp.einsum('bhcd,bhce->bhde', kd_c, w)
        return S, o

    nb, bb, hb = q.shape[:3]
    S0 = jnp.zeros((bb, hb, q.shape[-1], v.shape[-1]), jnp.float32)
    _, o = lax.scan(step, S0, (u, wk, attn, qg, kd, decay))
    return from_chunks(o)


def mlstm_chunked(q, k, v, ig, lf):
    q, k, v, ig, lf = (to_chunks(t) for t in (q, k, v, ig, lf))
    incl = jnp.tril(jnp.ones((CHUNK, CHUNK), bool))
    b = jnp.cumsum(lf, -1)
    D = jnp.where(incl, b[..., :, None] - b[..., None, :] + ig[..., None, :], -jnp.inf)
    g_end = b[..., -1:] - b + ig
    b_end = b[..., -1]

    def step(carry, xs):
        S, n, m = carry
        q_c, k_c, v_c, b_c, D_c, ge_c, be_c = xs
        m_t = jnp.maximum(b_c + m[..., None], jnp.max(D_c, -1))
        inter = jnp.exp(b_c + m[..., None] - m_t)
        A = jnp.einsum('bhid,bhjd->bhij', q_c, k_c) * jnp.exp(D_c - m_t[..., None])
        num = inter[..., None] * jnp.einsum('bhcd,bhde->bhce', q_c, S) + jnp.einsum('bhij,bhje->bhie', A, v_c)
        nq = inter * jnp.einsum('bhcd,bhd->bhc', q_c, n) + jnp.sum(A, -1)
        h = num / jnp.maximum(jnp.abs(nq), jnp.exp(-m_t))[..., None]
        m_new = jnp.maximum(be_c + m, jnp.max(ge_c, -1))
        w_state = jnp.exp(be_c + m - m_new)
        w_key = jnp.exp(ge_c - m_new[..., None])
        S = w_state[..., None, None] * S + jnp.einsum('bhcd,bhce->bhde', k_c * w_key[..., None], v_c)
        n = w_state[..., None] * n + jnp.einsum('bhcd,bhc->bhd', k_c, w_key)
        return (S, n, m_new), h

    bb, hb = q.shape[1:3]
    carry0 = (jnp.zeros((bb, hb, q.shape[-1], v.shape[-1]), jnp.float32),
              jnp.zeros((bb, hb, q.shape[-1]), jnp.float32),
              jnp.zeros((bb, hb), jnp.float32))
    _, h = lax.scan(step, carry0, (q, k, v, b, D, g_end, b_end))
    return from_chunks(h)


def gdn_mixer(h, w_in, conv_w, a_log, dt_bias, norm_w, w_out):
    B, T, _ = h.shape
    p = h @ w_in
    o1 = 2 * QK_W + D_INNER
    qkv, z, a, bt = jnp.split(p, [o1, o1 + D_INNER, o1 + D_INNER + N_HEADS], -1)
    qkv = jax.nn.silu(causal_dwconv(qkv, conv_w))
    q, k, v = jnp.split(qkv.astype(jnp.float32), [QK_W, 2 * QK_W], -1)
    q = l2norm(q.reshape(B, T, N_HEADS, HEAD_QK)) * (HEAD_QK ** -0.5)
    k = l2norm(k.reshape(B, T, N_HEADS, HEAD_QK))
    v = v.reshape(B, T, N_HEADS, HEAD_V)
    g = -jnp.exp(a_log.astype(jnp.float32)) * jax.nn.softplus(a.astype(jnp.float32) + dt_bias.astype(jnp.float32))
    beta = jax.nn.sigmoid(bt.astype(jnp.float32))
    o = gated_delta_chunked(q, k, v, g, beta)
    o = rmsnorm(o, norm_w) * jax.nn.silu(z.astype(jnp.float32).reshape(B, T, N_HEADS, HEAD_V))
    return o.reshape(B, T, D_INNER).astype(h.dtype) @ w_out


def mlstm_mixer(h, w_in, conv_w, i_bias, f_bias, norm_w, w_out):
    B, T, _ = h.shape
    p = h @ w_in
    c0 = 2 * QK_W
    qk, v, og, z, ig, fg = jnp.split(p, [c0, c0 + D_INNER, c0 + 2 * D_INNER, c0 + 3 * D_INNER,
                                         c0 + 3 * D_INNER + N_HEADS], -1)
    qk = jax.nn.silu(causal_dwconv(qk, conv_w))
    q, k = jnp.split(qk.astype(jnp.float32), [QK_W], -1)
    q = q.reshape(B, T, N_HEADS, HEAD_QK)
    k = k.reshape(B, T, N_HEADS, HEAD_QK) * (HEAD_QK ** -0.5)
    v = v.astype(jnp.float32).reshape(B, T, N_HEADS, HEAD_V)
    i_pre = ig.astype(jnp.float32) + i_bias.astype(jnp.float32)
    lf = jax.nn.log_sigmoid(fg.astype(jnp.float32) + f_bias.astype(jnp.float32))
    hc = mlstm_chunked(q, k, v, i_pre, lf)
    gate = jax.nn.sigmoid(og.astype(jnp.float32)) * jax.nn.silu(z.astype(jnp.float32))
    hc = rmsnorm(hc, norm_w).reshape(B, T, D_INNER) * gate
    return hc.astype(h.dtype) @ w_out


def setup_inputs(seed: int = 0) -> dict:
    key = jax.random.key(seed)
    ks = jax.random.split(key, 16)
    nrm = jax.random.normal
    x = nrm(ks[0], (BATCH, SEQ, D_MODEL), jnp.float32)
    norm_w = 1.0 + 0.02 * nrm(ks[1], (DEPTH, D_MODEL), jnp.float32)
    final_norm_w = 1.0 + 0.02 * nrm(ks[2], (D_MODEL,), jnp.float32)
    gdn_w_in = nrm(ks[3], (N_GDN, D_MODEL, GDN_IN), jnp.float32) * D_MODEL ** -0.5
    gdn_conv_w = nrm(ks[4], (N_GDN, CONV_K, 2 * QK_W + D_INNER), jnp.float32) * CONV_K ** -0.5
    gdn_a_log = jnp.log(jax.random.uniform(ks[5], (N_GDN, N_HEADS), jnp.float32, 1.0, 16.0))
    dt = jnp.exp(jax.random.uniform(ks[6], (N_GDN, N_HEADS), jnp.float32, jnp.log(1e-3), jnp.log(1e-1)))
    gdn_dt_bias = dt + jnp.log(-jnp.expm1(-dt))
    gdn_norm_w = 1.0 + 0.02 * nrm(ks[7], (N_GDN, HEAD_V), jnp.float32)
    gdn_w_out = nrm(ks[8], (N_GDN, D_INNER, D_MODEL), jnp.float32) * D_INNER ** -0.5
    mlstm_w_in = nrm(ks[9], (N_MLSTM, D_MODEL, MLSTM_IN), jnp.float32) * D_MODEL ** -0.5
    mlstm_conv_w = nrm(ks[10], (N_MLSTM, CONV_K, 2 * QK_W), jnp.float32) * CONV_K ** -0.5
    mlstm_i_bias = 0.1 * nrm(ks[11], (N_MLSTM, N_HEADS), jnp.float32)
    mlstm_f_bias = jnp.linspace(3.0, 6.0, N_HEADS, dtype=jnp.float32)[None, :] + 0.1 * nrm(ks[12], (N_MLSTM, N_HEADS), jnp.float32)
    mlstm_norm_w = 1.0 + 0.02 * nrm(ks[13], (N_MLSTM, HEAD_V), jnp.float32)
    mlstm_w_out = nrm(ks[14], (N_MLSTM, D_INNER, D_MODEL), jnp.float32) * D_INNER ** -0.5
    return {'x': x, 'norm_w': norm_w, 'final_norm_w': final_norm_w,
            'gdn_w_in': gdn_w_in, 'gdn_conv_w': gdn_conv_w, 'gdn_a_log': gdn_a_log, 'gdn_dt_bias': gdn_dt_bias,
            'gdn_norm_w': gdn_norm_w, 'gdn_w_out': gdn_w_out,
            'mlstm_w_in': mlstm_w_in, 'mlstm_conv_w': mlstm_conv_w, 'mlstm_i_bias': mlstm_i_bias,
            'mlstm_f_bias': mlstm_f_bias, 'mlstm_norm_w': mlstm_norm_w, 'mlstm_w_out': mlstm_w_out}


def reference(x, norm_w, final_norm_w, gdn_w_in, gdn_conv_w, gdn_a_log, gdn_dt_bias, gdn_norm_w, gdn_w_out,
              mlstm_w_in, mlstm_conv_w, mlstm_i_bias, mlstm_f_bias, mlstm_norm_w, mlstm_w_out):
    for i in range(DEPTH):
        h = rmsnorm(x, norm_w[i])
        j = i // N_MIXERS
        if i % N_MIXERS == 0:
            y = gdn_mixer(h, gdn_w_in[j], gdn_conv_w[j], gdn_a_log[j], gdn_dt_bias[j], gdn_norm_w[j], gdn_w_out[j])
        else:
            y = mlstm_mixer(h, mlstm_w_in[j], mlstm_conv_w[j], mlstm_i_bias[j], mlstm_f_bias[j], mlstm_norm_w[j],
                            mlstm_w_out[j])
        x = x + y
    return rmsnorm(x, final_norm_w)
```

```python
import functools

import jax
import jax.numpy as jnp
from jax import lax
from jax.experimental import pallas as pl
from jax.experimental.pallas import tpu as pltpu

F32 = jnp.float32
BF16 = jnp.bfloat16
HIGHEST = lax.Precision.HIGHEST

D_MODEL = 1024
N_HEADS = 8
HEAD_QK = 128
HEAD_V = 256
D_INNER = N_HEADS * HEAD_V
QK_W = N_HEADS * HEAD_QK
CONV_K = 4
CHUNK = 64
NORM_EPS = 1e-6
GATE_PAD = 128

ROW_TILE = 512
COL_TILE = 1024
TIME_BLOCK = 512
CHUNKS_PER_BLOCK = TIME_BLOCK // CHUNK
CONV_HALO = 8
VMEM_LIMIT_BYTES = 48 * 1024 * 1024


def _dot(a, b):
    return jnp.dot(a.astype(BF16), b.astype(BF16), preferred_element_type=F32)


def _dot_nt(a, b):
    return lax.dot_general(a.astype(BF16), b.astype(BF16), (((1,), (1,)), ((), ())),
                           preferred_element_type=F32)


def _dot_tn(a, b):
    return lax.dot_general(a.astype(BF16), b.astype(BF16), (((0,), (0,)), ((), ())),
                           preferred_element_type=F32)


def _dot_f32(a, b):
    return jnp.dot(a, b, precision=HIGHEST, preferred_element_type=F32)


def _dot_nt_f32(a, b):
    return lax.dot_general(a, b, (((1,), (1,)), ((), ())), precision=HIGHEST,
                           preferred_element_type=F32)


def _sigmoid(x):
    return jax.nn.sigmoid(x)


def _silu(x):
    return x * _sigmoid(x)


def _softplus(x):
    return jnp.maximum(x, 0.0) + jnp.log1p(jnp.exp(-jnp.abs(x)))


def _rms(x, w):
    return x * lax.rsqrt(jnp.mean(x * x, axis=-1, keepdims=True) + NORM_EPS) * w


def _inproj_kernel(x_ref, nw_ref, w_ref, wg_ref, p_ref, pg_ref, hn_ref):
    @pl.when(pl.program_id(1) == 0)
    def _():
        hn = _rms(x_ref[...], nw_ref[...]).astype(BF16)
        hn_ref[...] = hn
        pg_ref[...] = jnp.dot(hn, wg_ref[...], preferred_element_type=F32)

    p_ref[...] = jnp.dot(hn_ref[...], w_ref[...], preferred_element_type=F32)


def _inproj(x2d, nw, w_main, w_gate):
    m, d = x2d.shape
    n = w_main.shape[1]
    return pl.pallas_call(
        _inproj_kernel,
        grid=(m // ROW_TILE, n // COL_TILE),
        in_specs=[
            pl.BlockSpec((ROW_TILE, d), lambda i, j: (i, 0)),
            pl.BlockSpec((1, d), lambda i, j: (0, 0)),
            pl.BlockSpec((d, COL_TILE), lambda i, j: (0, j)),
            pl.BlockSpec((d, GATE_PAD), lambda i, j: (0, 0)),
        ],
        out_specs=[
            pl.BlockSpec((ROW_TILE, COL_TILE), lambda i, j: (i, j)),
            pl.BlockSpec((ROW_TILE, GATE_PAD), lambda i, j: (i, 0)),
        ],
        out_shape=[jax.ShapeDtypeStruct((m, n), F32), jax.ShapeDtypeStruct((m, GATE_PAD), F32)],
        scratch_shapes=[pltpu.VMEM((ROW_TILE, d), BF16)],
        compiler_params=pltpu.CompilerParams(
            dimension_semantics=("parallel", "arbitrary"), vmem_limit_bytes=VMEM_LIMIT_BYTES),
        name="inproj",
    )(x2d, nw.reshape(1, d), w_main, w_gate)


def _outproj_kernel(g_ref, w_ref, x_ref, o_ref):
    o_ref[...] = x_ref[...] + jnp.dot(g_ref[...], w_ref[...], preferred_element_type=F32)


def _outproj_final_kernel(g_ref, w_ref, x_ref, fw_ref, o_ref):
    y = x_ref[...] + jnp.dot(g_ref[...], w_ref[...], preferred_element_type=F32)
    o_ref[...] = _rms(y, fw_ref[...])


def _outproj(g2d, w_out, x2d, final_w=None):
    m, d = x2d.shape
    k = g2d.shape[1]
    in_specs = [
        pl.BlockSpec((ROW_TILE, k), lambda i: (i, 0)),
        pl.BlockSpec((k, d), lambda i: (0, 0)),
        pl.BlockSpec((ROW_TILE, d), lambda i: (i, 0)),
    ]
    args = [g2d, w_out, x2d]
    body = _outproj_kernel
    if final_w is not None:
        in_specs.append(pl.BlockSpec((1, d), lambda i: (0, 0)))
        args.append(final_w.reshape(1, d))
        body = _outproj_final_kernel
    return pl.pallas_call(
        body,
        grid=(m // ROW_TILE,),
        in_specs=in_specs,
        out_specs=pl.BlockSpec((ROW_TILE, d), lambda i: (i, 0)),
        out_shape=jax.ShapeDtypeStruct((m, d), F32),
        compiler_params=pltpu.CompilerParams(
            dimension_semantics=("parallel",), vmem_limit_bytes=VMEM_LIMIT_BYTES),
        name="outproj_final" if final_w is not None else "outproj",
    )(*args)


def _causal_conv_silu(raw_ref, buf_ref, cw_ref):
    tb = TIME_BLOCK
    buf_ref[CONV_HALO:CONV_HALO + tb, :] = raw_ref[0]
    cw = cw_ref[...]
    acc = cw[0:1, :] * buf_ref[CONV_HALO - 3:CONV_HALO - 3 + tb, :]
    for j in range(1, CONV_K):
        lo = CONV_HALO - 3 + j
        acc = acc + cw[j:j + 1, :] * buf_ref[lo:lo + tb, :]
    buf_ref[0:CONV_HALO, :] = buf_ref[tb:tb + CONV_HALO, :]
    return _silu(acc)


def _chunk_iotas():
    r = lax.broadcasted_iota(jnp.int32, (CHUNK, CHUNK), 0)
    c = lax.broadcasted_iota(jnp.int32, (CHUNK, CHUNK), 1)
    return r, c


def _cumsum_and_columns(rows_list, r, c):
    eye = (r == c).astype(F32)
    pad = jnp.zeros((128 - CHUNKS_PER_BLOCK, CHUNK), F32)
    return [_dot_nt_f32(eye, jnp.concatenate([x, pad], axis=0)) for x in rows_list]


def _col(xt, n, width):
    return jnp.broadcast_to(xt[:, n:n + 1], (CHUNK, width))


def _unit_lower_inverse(a, r, c):
    eye = (r == c).astype(F32)
    base = 8
    a0 = jnp.where((r // base) == (c // base), a, 0.0)
    a2 = _dot_f32(a0, a0)
    a4 = _dot_f32(a2, a2)
    t = _dot_f32(_dot_f32(eye - a0, eye + a2), eye + a4)
    s = base
    while s < CHUNK:
        off = ((r // (2 * s)) == (c // (2 * s))) & (((r // s) % 2) == 1) & (((c // s) % 2) == 0)
        t = t - _dot_f32(t, _dot_f32(jnp.where(off, a, 0.0), t))
        s *= 2
    return t


def _gdn_kernel(alog_ref, dtb_ref, q_ref, k_ref, v_ref, z_ref, cwq_ref, cwk_ref, cwv_ref,
                a_ref, bt_ref, nw_ref, o_ref, qbuf, kbuf, vbuf, s_ref):
    h = pl.program_id(1)

    @pl.when(pl.program_id(2) == 0)
    def _():
        s_ref[...] = jnp.zeros_like(s_ref)
        qbuf[0:CONV_HALO, :] = jnp.zeros((CONV_HALO, HEAD_QK), F32)
        kbuf[0:CONV_HALO, :] = jnp.zeros((CONV_HALO, HEAD_QK), F32)
        vbuf[0:CONV_HALO, :] = jnp.zeros((CONV_HALO, HEAD_V), F32)

    q = _causal_conv_silu(q_ref, qbuf, cwq_ref)
    k = _causal_conv_silu(k_ref, kbuf, cwk_ref)
    v = _causal_conv_silu(v_ref, vbuf, cwv_ref)
    q = q * lax.rsqrt(jnp.sum(q * q, axis=-1, keepdims=True) + 1e-6) * (HEAD_QK ** -0.5)
    k = k * lax.rsqrt(jnp.sum(k * k, axis=-1, keepdims=True) + 1e-6)

    r, c = _chunk_iotas()
    incl = r >= c
    strict = r > c
    upper = (r <= c).astype(F32)

    neg_rate = -jnp.exp(jnp.full((1, 1), alog_ref[h], F32))
    g = neg_rate * _softplus(a_ref[0, 0] + dtb_ref[h])
    beta = _sigmoid(bt_ref[0, 0])
    gc = _dot_f32(g, upper)
    gc_t, beta_t = _cumsum_and_columns([gc, beta], r, c)

    nw = nw_ref[...]
    s = s_ref[...]
    for n in range(CHUNKS_PER_BLOCK):
        lo = n * CHUNK
        q_c, k_c, v_c = q[lo:lo + CHUNK], k[lo:lo + CHUNK], v[lo:lo + CHUNK]
        gc_row = gc[n:n + 1, :]
        gc_last = gc_row[:, CHUNK - 1:CHUNK]
        gc_col = _col(gc_t, n, CHUNK)
        decay = jnp.where(incl, jnp.exp(jnp.where(incl, gc_col - gc_row, 0.0)), 0.0)
        beta_qk = _col(beta_t, n, HEAD_QK)
        e_qk = jnp.exp(_col(gc_t, n, HEAD_QK))
        kk = _dot_nt(k_c, k_c)
        a = jnp.where(strict, beta_qk[:, :CHUNK] * kk * decay, 0.0)
        t = _unit_lower_inverse(a, r, c)
        rhs = jnp.concatenate([_col(beta_t, n, HEAD_V) * v_c, (beta_qk * e_qk) * k_c], axis=1)
        uw = _dot_f32(t, rhs)
        u, wk = uw[:, :HEAD_V], uw[:, HEAD_V:]
        attn = _dot_nt(q_c, k_c) * decay
        qg = q_c * e_qk
        kd = k_c * jnp.exp(gc_last - _col(gc_t, n, HEAD_QK))
        w = u - _dot(wk, s)
        o = _dot(qg, s) + _dot(attn, w)
        s = jnp.exp(gc_last) * s + _dot_tn(kd, w)
        out = _rms(o, nw) * _silu(z_ref[0, lo:lo + CHUNK, :])
        o_ref[0, lo:lo + CHUNK, :] = out.astype(o_ref.dtype)
    s_ref[...] = s


def _gdn_recurrence(p3, conv_w, a_r, bt_r, a_log, dt_bias, norm_w):
    b, t, _ = p3.shape
    tb = TIME_BLOCK
    nq = QK_W // HEAD_QK
    nv = 2 * QK_W // HEAD_V
    nz = (2 * QK_W + D_INNER) // HEAD_V
    smem = pl.BlockSpec(memory_space=pltpu.SMEM)
    return pl.pallas_call(
        _gdn_kernel,
        grid=(b, N_HEADS, t // tb),
        in_specs=[
            smem, smem,
            pl.BlockSpec((1, tb, HEAD_QK), lambda bi, h, ti: (bi, ti, h)),
            pl.BlockSpec((1, tb, HEAD_QK), lambda bi, h, ti: (bi, ti, nq + h)),
            pl.BlockSpec((1, tb, HEAD_V), lambda bi, h, ti: (bi, ti, nv + h)),
            pl.BlockSpec((1, tb, HEAD_V), lambda bi, h, ti: (bi, ti, nz + h)),
            pl.BlockSpec((CONV_K, HEAD_QK), lambda bi, h, ti: (0, h)),
            pl.BlockSpec((CONV_K, HEAD_QK), lambda bi, h, ti: (0, nq + h)),
            pl.BlockSpec((CONV_K, HEAD_V), lambda bi, h, ti: (0, nv + h)),
            pl.BlockSpec((1, 1, CHUNKS_PER_BLOCK, CHUNK), lambda bi, h, ti: (bi, h, ti, 0)),
            pl.BlockSpec((1, 1, CHUNKS_PER_BLOCK, CHUNK), lambda bi, h, ti: (bi, h, ti, 0)),
            pl.BlockSpec((1, HEAD_V), lambda bi, h, ti: (0, 0)),
        ],
        out_specs=pl.BlockSpec((1, tb, HEAD_V), lambda bi, h, ti: (bi, ti, h)),
        out_shape=jax.ShapeDtypeStruct((b, t, D_INNER), BF16),
        scratch_shapes=[
            pltpu.VMEM((tb + CONV_HALO, HEAD_QK), F32),
            pltpu.VMEM((tb + CONV_HALO, HEAD_QK), F32),
            pltpu.VMEM((tb + CONV_HALO, HEAD_V), F32),
            pltpu.VMEM((HEAD_QK, HEAD_V), F32),
        ],
        compiler_params=pltpu.CompilerParams(
            dimension_semantics=("parallel", "parallel", "arbitrary"),
            vmem_limit_bytes=VMEM_LIMIT_BYTES),
        name="gdn_recurrence",
    )(a_log, dt_bias, p3, p3, p3, p3, conv_w, conv_w, conv_w, a_r, bt_r, norm_w.reshape(1, HEAD_V))


def _mlstm_kernel(ib_ref, fb_ref, q_ref, k_ref, v_ref, og_ref, z_ref, cwq_ref, cwk_ref,
                  ig_ref, fg_ref, nw_ref, o_ref, qbuf, kbuf, s_ref, m_ref):
    h = pl.program_id(1)

    @pl.when(pl.program_id(2) == 0)
    def _():
        s_ref[...] = jnp.zeros_like(s_ref)
        m_ref[...] = jnp.zeros_like(m_ref)
        qbuf[0:CONV_HALO, :] = jnp.zeros((CONV_HALO, HEAD_QK), F32)
        kbuf[0:CONV_HALO, :] = jnp.zeros((CONV_HALO, HEAD_QK), F32)

    q = _causal_conv_silu(q_ref, qbuf, cwq_ref)
    k = _causal_conv_silu(k_ref, kbuf, cwk_ref) * (HEAD_QK ** -0.5)
    v = v_ref[0]

    r, c = _chunk_iotas()
    incl = r >= c
    upper = (r <= c).astype(F32)
    ones_col = (lax.broadcasted_iota(jnp.int32, (CHUNK, 128), 1) == 0).astype(F32)

    i_pre = ig_ref[0, 0] + ib_ref[h]
    lf = -_softplus(-(fg_ref[0, 0] + fb_ref[h]))
    bc = _dot_f32(lf, upper)
    bc_t, ip_t = _cumsum_and_columns([bc, i_pre], r, c)

    nw = nw_ref[...]
    s = s_ref[...]
    m = m_ref[0:1, 0:1]
    for n in range(CHUNKS_PER_BLOCK):
        lo = n * CHUNK
        q_c, k_c = q[lo:lo + CHUNK], k[lo:lo + CHUNK]
        v_ext = jnp.concatenate([v[lo:lo + CHUNK], ones_col], axis=1)
        b_row, i_row = bc[n:n + 1, :], i_pre[n:n + 1, :]
        b_last = b_row[:, CHUNK - 1:CHUNK]
        b_col = bc_t[:, n:n + 1]
        i_col = ip_t[:, n:n + 1]
        dmat = jnp.where(incl, b_col - b_row + i_row, -jnp.inf)
        rmax = jnp.max(dmat, axis=-1, keepdims=True)
        ge_row = b_last - b_row + i_row
        ge_col = b_last - b_col + i_col
        gmax = jnp.max(ge_row, axis=-1, keepdims=True)

        m_t = jnp.maximum(b_col + m, rmax)
        inter = jnp.exp(b_col + m - m_t)
        amat = _dot_nt(q_c, k_c) * jnp.exp(dmat - m_t)
        num_ext = inter * _dot(q_c, s) + _dot(amat, v_ext)
        nq = num_ext[:, HEAD_V:HEAD_V + 1]
        hh = num_ext[:, :HEAD_V] / jnp.maximum(jnp.abs(nq), jnp.exp(-m_t))
        m_new = jnp.maximum(b_last + m, gmax)
        w_state = jnp.exp(b_last + m - m_new)
        w_key = jnp.exp(ge_col - m_new)
        s = w_state * s + _dot_tn(k_c * w_key, v_ext)
        m = m_new

        gate = _sigmoid(og_ref[0, lo:lo + CHUNK, :]) * _silu(z_ref[0, lo:lo + CHUNK, :])
        o_ref[0, lo:lo + CHUNK, :] = (_rms(hh, nw) * gate).astype(o_ref.dtype)
    s_ref[...] = s
    m_ref[...] = jnp.broadcast_to(m, m_ref.shape)


def _mlstm_recurrence(p3, conv_w, ig_r, fg_r, i_bias, f_bias, norm_w):
    b, t, _ = p3.shape
    tb = TIME_BLOCK
    nk = QK_W // HEAD_QK
    nv = 2 * QK_W // HEAD_V
    nh = D_INNER // HEAD_V
    smem = pl.BlockSpec(memory_space=pltpu.SMEM)
    return pl.pallas_call(
        _mlstm_kernel,
        grid=(b, N_HEADS, t // tb),
        in_specs=[
            smem, smem,
            pl.BlockSpec((1, tb, HEAD_QK), lambda bi, h, ti: (bi, ti, h)),
            pl.BlockSpec((1, tb, HEAD_QK), lambda bi, h, ti: (bi, ti, nk + h)),
            pl.BlockSpec((1, tb, HEAD_V), lambda bi, h, ti: (bi, ti, nv + h)),
            pl.BlockSpec((1, tb, HEAD_V), lambda bi, h, ti: (bi, ti, nv + nh + h)),
            pl.BlockSpec((1, tb, HEAD_V), lambda bi, h, ti: (bi, ti, nv + 2 * nh + h)),
            pl.BlockSpec((CONV_K, HEAD_QK), lambda bi, h, ti: (0, h)),
            pl.BlockSpec((CONV_K, HEAD_QK), lambda bi, h, ti: (0, nk + h)),
            pl.BlockSpec((1, 1, CHUNKS_PER_BLOCK, CHUNK), lambda bi, h, ti: (bi, h, ti, 0)),
            pl.BlockSpec((1, 1, CHUNKS_PER_BLOCK, CHUNK), lambda bi, h, ti: (bi, h, ti, 0)),
            pl.BlockSpec((1, HEAD_V), lambda bi, h, ti: (0, 0)),
        ],
        out_specs=pl.BlockSpec((1, tb, HEAD_V), lambda bi, h, ti: (bi, ti, h)),
        out_shape=jax.ShapeDtypeStruct((b, t, D_INNER), BF16),
        scratch_shapes=[
            pltpu.VMEM((tb + CONV_HALO, HEAD_QK), F32),
            pltpu.VMEM((tb + CONV_HALO, HEAD_QK), F32),
            pltpu.VMEM((HEAD_QK, HEAD_V + 128), F32),
            pltpu.VMEM((8, 128), F32),
        ],
        compiler_params=pltpu.CompilerParams(
            dimension_semantics=("parallel", "parallel", "arbitrary"),
            vmem_limit_bytes=VMEM_LIMIT_BYTES),
        name="mlstm_recurrence",
    )(i_bias, f_bias, p3, p3, p3, p3, p3, conv_w, conv_w, ig_r, fg_r, norm_w.reshape(1, HEAD_V))


def _split_in_weight(w_in):
    n_main = w_in.shape[1] - 2 * N_HEADS
    w_main = w_in[:, :n_main].astype(BF16)
    w_gate = jnp.pad(w_in[:, n_main:], ((0, 0), (0, GATE_PAD - 2 * N_HEADS))).astype(BF16)
    return w_main, w_gate


def _gate_rows(pg, lo, b, t):
    g = pg[:, lo:lo + N_HEADS].reshape(b, t, N_HEADS)
    return jnp.transpose(g, (0, 2, 1)).reshape(b, N_HEADS, t // CHUNK, CHUNK)


def kernel(x, norm_w, final_norm_w, gdn_w_in, gdn_conv_w, gdn_a_log, gdn_dt_bias, gdn_norm_w, gdn_w_out,
           mlstm_w_in, mlstm_conv_w, mlstm_i_bias, mlstm_f_bias, mlstm_norm_w, mlstm_w_out):
    b, t, d = x.shape
    depth = norm_w.shape[0]
    x2d = x.reshape(b * t, d)
    for i in range(depth):
        j = i // 2
        last = final_norm_w if i == depth - 1 else None
        if i % 2 == 0:
            w_main, w_gate = _split_in_weight(gdn_w_in[j])
            p, pg = _inproj(x2d, norm_w[i], w_main, w_gate)
            g = _gdn_recurrence(p.reshape(b, t, -1), gdn_conv_w[j], _gate_rows(pg, 0, b, t),
                                _gate_rows(pg, N_HEADS, b, t), gdn_a_log[j], gdn_dt_bias[j], gdn_norm_w[j])
            x2d = _outproj(g.reshape(b * t, D_INNER), gdn_w_out[j].astype(BF16), x2d, last)
        else:
            w_main, w_gate = _split_in_weight(mlstm_w_in[j])
            p, pg = _inproj(x2d, norm_w[i], w_main, w_gate)
            g = _mlstm_recurrence(p.reshape(b, t, -1), mlstm_conv_w[j], _gate_rows(pg, 0, b, t),
                                  _gate_rows(pg, N_HEADS, b, t), mlstm_i_bias[j], mlstm_f_bias[j],
                                  mlstm_norm_w[j])
            x2d = _outproj(g.reshape(b * t, D_INNER), mlstm_w_out[j].astype(BF16), x2d, last)
    return x2d.reshape(b, t, d)
```

```python
import functools

import jax
import jax.numpy as jnp
from jax import lax
from jax.experimental import pallas as pl
from jax.experimental.pallas import tpu as pltpu

F32 = jnp.float32
BF16 = jnp.bfloat16
HIGHEST = lax.Precision.HIGHEST

D_MODEL = 1024
N_HEADS = 8
HEAD_QK = 128
HEAD_V = 256
D_INNER = N_HEADS * HEAD_V
QK_W = N_HEADS * HEAD_QK
CONV_K = 4
CHUNK = 64
NORM_EPS = 1e-6
GATE_PAD = 128

ROW_TILE = 512
COL_TILE = 1024
TIME_BLOCK = 512
CHUNKS_PER_BLOCK = TIME_BLOCK // CHUNK
CONV_HALO = 8
VMEM_LIMIT_BYTES = 48 * 1024 * 1024


def _dot(a, b):
    return jnp.dot(a.astype(BF16), b.astype(BF16), preferred_element_type=F32)


def _dot_nt(a, b):
    return lax.dot_general(a.astype(BF16), b.astype(BF16), (((1,), (1,)), ((), ())),
                           preferred_element_type=F32)


def _dot_tn(a, b):
    return lax.dot_general(a.astype(BF16), b.astype(BF16), (((0,), (0,)), ((), ())),
                           preferred_element_type=F32)


def _dot_f32(a, b):
    return jnp.dot(a, b, precision=HIGHEST, preferred_element_type=F32)


def _dot_nt_f32(a, b):
    return lax.dot_general(a, b, (((1,), (1,)), ((), ())), precision=HIGHEST,
                           preferred_element_type=F32)


def _sigmoid(x):
    return jax.nn.sigmoid(x)


def _silu(x):
    return x * _sigmoid(x)


def _softplus(x):
    return jnp.maximum(x, 0.0) + jnp.log1p(jnp.exp(-jnp.abs(x)))


def _rms(x, w):
    return x * lax.rsqrt(jnp.mean(x * x, axis=-1, keepdims=True) + NORM_EPS) * w


def _inproj_kernel(x_ref, nw_ref, w_ref, wg_ref, p_ref, pg_ref, hn_ref):
    @pl.when(pl.program_id(1) == 0)
    def _():
        hn = _rms(x_ref[...], nw_ref[...]).astype(BF16)
        hn_ref[...] = hn
        pg_ref[...] = jnp.dot(hn, wg_ref[...], preferred_element_type=F32)

    p_ref[...] = jnp.dot(hn_ref[...], w_ref[...], preferred_element_type=F32)


def _inproj(x2d, nw, w_main, w_gate):
    m, d = x2d.shape
    n = w_main.shape[1]
    return pl.pallas_call(
        _inproj_kernel,
        grid=(m // ROW_TILE, n // COL_TILE),
        in_specs=[
            pl.BlockSpec((ROW_TILE, d), lambda i, j: (i, 0)),
            pl.BlockSpec((1, d), lambda i, j: (0, 0)),
            pl.BlockSpec((d, COL_TILE), lambda i, j: (0, j)),
            pl.BlockSpec((d, GATE_PAD), lambda i, j: (0, 0)),
        ],
        out_specs=[
            pl.BlockSpec((ROW_TILE, COL_TILE), lambda i, j: (i, j)),
            pl.BlockSpec((ROW_TILE, GATE_PAD), lambda i, j: (i, 0)),
        ],
        out_shape=[jax.ShapeDtypeStruct((m, n), F32), jax.ShapeDtypeStruct((m, GATE_PAD), F32)],
        scratch_shapes=[pltpu.VMEM((ROW_TILE, d), BF16)],
        compiler_params=pltpu.CompilerParams(
            dimension_semantics=("parallel", "arbitrary"), vmem_limit_bytes=VMEM_LIMIT_BYTES),
        name="inproj",
    )(x2d, nw.reshape(1, d), w_main, w_gate)


def _outproj_kernel(g_ref, w_ref, x_ref, o_ref):
    o_ref[...] = x_ref[...] + jnp.dot(g_ref[...], w_ref[...], preferred_element_type=F32)


def _outproj_final_kernel(g_ref, w_ref, x_ref, fw_ref, o_ref):
    y = x_ref[...] + jnp.dot(g_ref[...], w_ref[...], preferred_element_type=F32)
    o_ref[...] = _rms(y, fw_ref[...])


def _outproj(g2d, w_out, x2d, final_w=None):
    m, d = x2d.shape
    k = g2d.shape[1]
    in_specs = [
        pl.BlockSpec((ROW_TILE, k), lambda i: (i, 0)),
        pl.BlockSpec((k, d), lambda i: (0, 0)),
        pl.BlockSpec((ROW_TILE, d), lambda i: (i, 0)),
    ]
    args = [g2d, w_out, x2d]
    body = _outproj_kernel
    if final_w is not None:
        in_specs.append(pl.BlockSpec((1, d), lambda i: (0, 0)))
        args.append(final_w.reshape(1, d))
        body = _outproj_final_kernel
    return pl.pallas_call(
        body,
        grid=(m // ROW_TILE,),
        in_specs=in_specs,
        out_specs=pl.BlockSpec((ROW_TILE, d), lambda i: (i, 0)),
        out_shape=jax.ShapeDtypeStruct((m, d), F32),
        compiler_params=pltpu.CompilerParams(
            dimension_semantics=("parallel",), vmem_limit_bytes=VMEM_LIMIT_BYTES),
        name="outproj_final" if final_w is not None else "outproj",
    )(*args)


def _causal_conv_silu(raw_ref, buf_ref, cw_ref):
    tb = TIME_BLOCK
    buf_ref[CONV_HALO:CONV_HALO + tb, :] = raw_ref[0]
    cw = cw_ref[...]
    acc = cw[0:1, :] * buf_ref[CONV_HALO - 3:CONV_HALO - 3 + tb, :]
    for j in range(1, CONV_K):
        lo = CONV_HALO - 3 + j
        acc = acc + cw[j:j + 1, :] * buf_ref[lo:lo + tb, :]
    buf_ref[0:CONV_HALO, :] = buf_ref[tb:tb + CONV_HALO, :]
    return _silu(acc)


def _chunk_iotas():
    r = lax.broadcasted_iota(jnp.int32, (CHUNK, CHUNK), 0)
    c = lax.broadcasted_iota(jnp.int32, (CHUNK, CHUNK), 1)
    return r, c


def _cumsum_and_columns(rows_list, r, c):
    eye = (r == c).astype(F32)
    pad = jnp.zeros((128 - CHUNKS_PER_BLOCK, CHUNK), F32)
    return [_dot_nt_f32(eye, jnp.concatenate([x, pad], axis=0)) for x in rows_list]


def _col(xt, n, width):
    return jnp.broadcast_to(xt[:, n:n + 1], (CHUNK, width))


def _columns(rows_list):
    rows = jnp.concatenate(rows_list, axis=0)
    rows = jnp.concatenate([rows, jnp.zeros((128 - rows.shape[0], CHUNK), F32)], axis=0)
    rows = jnp.concatenate([rows, jnp.zeros((128, 128 - CHUNK), F32)], axis=1)
    return rows.T[:CHUNK, :]


def _unit_lower_inverse_minus_identity(a_list, r, c):
    base = 8
    blk = (r // base) == (c // base)
    a0 = [jnp.where(blk, a, 0.0) for a in a_list]
    a2 = [_dot(x, x) for x in a0]
    a34 = [_dot(jnp.concatenate([x, y], axis=0), y) for x, y in zip(a0, a2)]
    p = [y - x - z[:CHUNK] for x, y, z in zip(a0, a2, a34)]
    a4 = [z[CHUNK:] for z in a34]
    n = [x + y + _dot(x, y) for x, y in zip(p, a4)]
    s = base
    while s < CHUNK:
        off_mask = ((r // (2 * s)) == (c // (2 * s))) & (((r // s) % 2) == 1) & (((c // s) % 2) == 0)
        off = [jnp.where(off_mask, a, 0.0) for a in a_list]
        x = [o + _dot(o, m) for o, m in zip(off, n)]
        n = [m - y - _dot(m, y) for m, y in zip(n, x)]
        s *= 2
    return n


GDN_HEADS_PER_STEP = 2
_GDN_COLUMN_KINDS = 5


def _gdn_kernel(alog_ref, dtb_ref, q_ref, k_ref, v_ref, z_ref, cwq_ref, cwk_ref, cwv_ref,
                a_ref, bt_ref, nw_ref, o_ref, qbuf, kbuf, vbuf, s_ref):
    hp, nc = GDN_HEADS_PER_STEP, CHUNKS_PER_BLOCK
    hg = pl.program_id(1)

    @pl.when(pl.program_id(2) == 0)
    def _():
        s_ref[...] = jnp.zeros_like(s_ref)
        qbuf[0:CONV_HALO, :] = jnp.zeros((CONV_HALO, hp * HEAD_QK), F32)
        kbuf[0:CONV_HALO, :] = jnp.zeros((CONV_HALO, hp * HEAD_QK), F32)
        vbuf[0:CONV_HALO, :] = jnp.zeros((CONV_HALO, hp * HEAD_V), F32)

    q = _causal_conv_silu(q_ref, qbuf, cwq_ref)
    k = _causal_conv_silu(k_ref, kbuf, cwk_ref)
    v = _causal_conv_silu(v_ref, vbuf, cwv_ref)

    r, c = _chunk_iotas()
    incl = r >= c
    strict = r > c
    upper = (r <= c).astype(F32)

    qh, kh, vh, gcs, es, rows = [], [], [], [], [], []
    for hd in range(hp):
        h = hg * hp + hd
        qs = q[:, hd * HEAD_QK:(hd + 1) * HEAD_QK]
        ks = k[:, hd * HEAD_QK:(hd + 1) * HEAD_QK]
        qh.append(qs * lax.rsqrt(jnp.sum(qs * qs, axis=-1, keepdims=True) + 1e-6) * (HEAD_QK ** -0.5))
        kh.append(ks * lax.rsqrt(jnp.sum(ks * ks, axis=-1, keepdims=True) + 1e-6))
        vh.append(v[:, hd * HEAD_V:(hd + 1) * HEAD_V])
        neg_rate = -jnp.exp(jnp.full((1, 1), alog_ref[h], F32))
        g = neg_rate * _softplus(a_ref[0, hd] + dtb_ref[h])
        beta = _sigmoid(bt_ref[0, hd])
        gc = _dot_f32(g, upper)
        e = jnp.exp(gc)
        gcs.append(gc)
        es.append(e)
        rows += [gc, beta, beta * e, e, jnp.exp(gc[:, CHUNK - 1:CHUNK] - gc)]
    cols = _columns(rows)

    def col(hd, kind, n, width):
        return _col(cols, (hd * _GDN_COLUMN_KINDS + kind) * nc + n, width)

    chains = [(hd, n) for n in range(nc) for hd in range(hp)]
    q_c = {ch: qh[ch[0]][ch[1] * CHUNK:(ch[1] + 1) * CHUNK] for ch in chains}
    k_c = {ch: kh[ch[0]][ch[1] * CHUNK:(ch[1] + 1) * CHUNK] for ch in chains}
    v_c = {ch: vh[ch[0]][ch[1] * CHUNK:(ch[1] + 1) * CHUNK] for ch in chains}

    kq = {ch: _dot_nt(jnp.concatenate([k_c[ch], q_c[ch]], axis=0), k_c[ch]) for ch in chains}
    a_list, attn = [], {}
    for ch in chains:
        hd, n = ch
        gc_row = gcs[hd][n:n + 1, :]
        decay = jnp.where(incl, jnp.exp(jnp.where(incl, col(hd, 0, n, CHUNK) - gc_row, 0.0)), 0.0)
        a_list.append(jnp.where(strict, col(hd, 1, n, CHUNK) * kq[ch][:CHUNK] * decay, 0.0))
        attn[ch] = (kq[ch][CHUNK:] * decay).astype(BF16)
    n_list = _unit_lower_inverse_minus_identity(a_list, r, c)
    u, lhs1, kd = {}, {}, {}
    for ch, nmat in zip(chains, n_list):
        hd, n = ch
        rhs = jnp.concatenate([col(hd, 1, n, HEAD_V) * v_c[ch], col(hd, 2, n, HEAD_QK) * k_c[ch]], axis=1)
        uw = rhs + _dot(nmat, rhs)
        u[ch] = uw[:, :HEAD_V]
        lhs1[ch] = jnp.concatenate([uw[:, HEAD_V:], q_c[ch] * col(hd, 3, n, HEAD_QK)], axis=0).astype(BF16)
        kd[ch] = (k_c[ch] * col(hd, 4, n, HEAD_QK)).astype(BF16)

    nw = nw_ref[...]
    s = [s_ref[hd] for hd in range(hp)]
    for n in range(nc):
        lo = n * CHUNK
        m1 = [_dot(lhs1[(hd, n)], s[hd]) for hd in range(hp)]
        w = [u[(hd, n)] - m1[hd][:CHUNK] for hd in range(hp)]
        o = [m1[hd][CHUNK:] + _dot(attn[(hd, n)], w[hd]) for hd in range(hp)]
        s = [es[hd][n:n + 1, CHUNK - 1:CHUNK] * s[hd] + _dot_tn(kd[(hd, n)], w[hd]) for hd in range(hp)]
        for hd in range(hp):
            z = z_ref[0, lo:lo + CHUNK, hd * HEAD_V:(hd + 1) * HEAD_V]
            o_ref[0, lo:lo + CHUNK, hd * HEAD_V:(hd + 1) * HEAD_V] = (_rms(o[hd], nw) * _silu(z)).astype(o_ref.dtype)
    for hd in range(hp):
        s_ref[hd] = s[hd]


def _gdn_recurrence(p3, conv_w, a_r, bt_r, a_log, dt_bias, norm_w):
    b, t, _ = p3.shape
    tb, hp = TIME_BLOCK, GDN_HEADS_PER_STEP
    wq, wv = hp * HEAD_QK, hp * HEAD_V
    nk = QK_W // wq
    nv = 2 * QK_W // wv
    nz = (2 * QK_W + D_INNER) // wv
    smem = pl.BlockSpec(memory_space=pltpu.SMEM)
    return pl.pallas_call(
        _gdn_kernel,
        grid=(b, N_HEADS // hp, t // tb),
        in_specs=[
            smem, smem,
            pl.BlockSpec((1, tb, wq), lambda bi, h, ti: (bi, ti, h)),
            pl.BlockSpec((1, tb, wq), lambda bi, h, ti: (bi, ti, nk + h)),
            pl.BlockSpec((1, tb, wv), lambda bi, h, ti: (bi, ti, nv + h)),
            pl.BlockSpec((1, tb, wv), lambda bi, h, ti: (bi, ti, nz + h)),
            pl.BlockSpec((CONV_K, wq), lambda bi, h, ti: (0, h)),
            pl.BlockSpec((CONV_K, wq), lambda bi, h, ti: (0, nk + h)),
            pl.BlockSpec((CONV_K, wv), lambda bi, h, ti: (0, nv + h)),
            pl.BlockSpec((1, hp, CHUNKS_PER_BLOCK, CHUNK), lambda bi, h, ti: (bi, h, ti, 0)),
            pl.BlockSpec((1, hp, CHUNKS_PER_BLOCK, CHUNK), lambda bi, h, ti: (bi, h, ti, 0)),
            pl.BlockSpec((1, HEAD_V), lambda bi, h, ti: (0, 0)),
        ],
        out_specs=pl.BlockSpec((1, tb, wv), lambda bi, h, ti: (bi, ti, h)),
        out_shape=jax.ShapeDtypeStruct((b, t, D_INNER), BF16),
        scratch_shapes=[
            pltpu.VMEM((tb + CONV_HALO, wq), F32),
            pltpu.VMEM((tb + CONV_HALO, wq), F32),
            pltpu.VMEM((tb + CONV_HALO, wv), F32),
            pltpu.VMEM((hp, HEAD_QK, HEAD_V), F32),
        ],
        compiler_params=pltpu.CompilerParams(
            dimension_semantics=("parallel", "parallel", "arbitrary"),
            vmem_limit_bytes=VMEM_LIMIT_BYTES),
        name="gdn_recurrence",
    )(a_log, dt_bias, p3, p3, p3, p3, conv_w, conv_w, conv_w, a_r, bt_r, norm_w.reshape(1, HEAD_V))


def _mlstm_kernel(ib_ref, fb_ref, q_ref, k_ref, v_ref, og_ref, z_ref, cwq_ref, cwk_ref,
                  ig_ref, fg_ref, nw_ref, o_ref, qbuf, kbuf, s_ref, m_ref):
    h = pl.program_id(1)

    @pl.when(pl.program_id(2) == 0)
    def _():
        s_ref[...] = jnp.zeros_like(s_ref)
        m_ref[...] = jnp.zeros_like(m_ref)
        qbuf[0:CONV_HALO, :] = jnp.zeros((CONV_HALO, HEAD_QK), F32)
        kbuf[0:CONV_HALO, :] = jnp.zeros((CONV_HALO, HEAD_QK), F32)

    q = _causal_conv_silu(q_ref, qbuf, cwq_ref)
    k = _causal_conv_silu(k_ref, kbuf, cwk_ref) * (HEAD_QK ** -0.5)
    v = v_ref[0]

    r, c = _chunk_iotas()
    incl = r >= c
    upper = (r <= c).astype(F32)
    ones_col = (lax.broadcasted_iota(jnp.int32, (CHUNK, 128), 1) == 0).astype(F32)

    i_pre = ig_ref[0, 0] + ib_ref[h]
    lf = -_softplus(-(fg_ref[0, 0] + fb_ref[h]))
    bc = _dot_f32(lf, upper)
    bc_t, ip_t = _cumsum_and_columns([bc, i_pre], r, c)

    nw = nw_ref[...]
    s = s_ref[...]
    m = m_ref[0:1, 0:1]
    for n in range(CHUNKS_PER_BLOCK):
        lo = n * CHUNK
        q_c, k_c = q[lo:lo + CHUNK], k[lo:lo + CHUNK]
        v_ext = jnp.concatenate([v[lo:lo + CHUNK], ones_col], axis=1)
        b_row, i_row = bc[n:n + 1, :], i_pre[n:n + 1, :]
        b_last = b_row[:, CHUNK - 1:CHUNK]
        b_col = bc_t[:, n:n + 1]
        i_col = ip_t[:, n:n + 1]
        dmat = jnp.where(incl, b_col - b_row + i_row, -jnp.inf)
        rmax = jnp.max(dmat, axis=-1, keepdims=True)
        ge_row = b_last - b_row + i_row
        ge_col = b_last - b_col + i_col
        gmax = jnp.max(ge_row, axis=-1, keepdims=True)

        m_t = jnp.maximum(b_col + m, rmax)
        inter = jnp.exp(b_col + m - m_t)
        amat = _dot_nt(q_c, k_c) * jnp.exp(dmat - m_t)
        num_ext = inter * _dot(q_c, s) + _dot(amat, v_ext)
        nq = num_ext[:, HEAD_V:HEAD_V + 1]
        hh = num_ext[:, :HEAD_V] / jnp.maximum(jnp.abs(nq), jnp.exp(-m_t))
        m_new = jnp.maximum(b_last + m, gmax)
        w_state = jnp.exp(b_last + m - m_new)
        w_key = jnp.exp(ge_col - m_new)
        s = w_state * s + _dot_tn(k_c * w_key, v_ext)
        m = m_new

        gate = _sigmoid(og_ref[0, lo:lo + CHUNK, :]) * _silu(z_ref[0, lo:lo + CHUNK, :])
        o_ref[0, lo:lo + CHUNK, :] = (_rms(hh, nw) * gate).astype(o_ref.dtype)
    s_ref[...] = s
    m_ref[...] = jnp.broadcast_to(m, m_ref.shape)


def _mlstm_recurrence(p3, conv_w, ig_r, fg_r, i_bias, f_bias, norm_w):
    b, t, _ = p3.shape
    tb = TIME_BLOCK
    nk = QK_W // HEAD_QK
    nv = 2 * QK_W // HEAD_V
    nh = D_INNER // HEAD_V
    smem = pl.BlockSpec(memory_space=pltpu.SMEM)
    return pl.pallas_call(
        _mlstm_kernel,
        grid=(b, N_HEADS, t // tb),
        in_specs=[
            smem, smem,
            pl.BlockSpec((1, tb, HEAD_QK), lambda bi, h, ti: (bi, ti, h)),
            pl.BlockSpec((1, tb, HEAD_QK), lambda bi, h, ti: (bi, ti, nk + h)),
            pl.BlockSpec((1, tb, HEAD_V), lambda bi, h, ti: (bi, ti, nv + h)),
            pl.BlockSpec((1, tb, HEAD_V), lambda bi, h, ti: (bi, ti, nv + nh + h)),
            pl.BlockSpec((1, tb, HEAD_V), lambda bi, h, ti: (bi, ti, nv + 2 * nh + h)),
            pl.BlockSpec((CONV_K, HEAD_QK), lambda bi, h, ti: (0, h)),
            pl.BlockSpec((CONV_K, HEAD_QK), lambda bi, h, ti: (0, nk + h)),
            pl.BlockSpec((1, 1, CHUNKS_PER_BLOCK, CHUNK), lambda bi, h, ti: (bi, h, ti, 0)),
            pl.BlockSpec((1, 1, CHUNKS_PER_BLOCK, CHUNK), lambda bi, h, ti: (bi, h, ti, 0)),
            pl.BlockSpec((1, HEAD_V), lambda bi, h, ti: (0, 0)),
        ],
        out_specs=pl.BlockSpec((1, tb, HEAD_V), lambda bi, h, ti: (bi, ti, h)),
        out_shape=jax.ShapeDtypeStruct((b, t, D_INNER), BF16),
        scratch_shapes=[
            pltpu.VMEM((tb + CONV_HALO, HEAD_QK), F32),
            pltpu.VMEM((tb + CONV_HALO, HEAD_QK), F32),
            pltpu.VMEM((HEAD_QK, HEAD_V + 128), F32),
            pltpu.VMEM((8, 128), F32),
        ],
        compiler_params=pltpu.CompilerParams(
            dimension_semantics=("parallel", "parallel", "arbitrary"),
            vmem_limit_bytes=VMEM_LIMIT_BYTES),
        name="mlstm_recurrence",
    )(i_bias, f_bias, p3, p3, p3, p3, p3, conv_w, conv_w, ig_r, fg_r, norm_w.reshape(1, HEAD_V))


def _split_in_weight(w_in):
    n_main = w_in.shape[1] - 2 * N_HEADS
    w_main = w_in[:, :n_main].astype(BF16)
    w_gate = jnp.pad(w_in[:, n_main:], ((0, 0), (0, GATE_PAD - 2 * N_HEADS))).astype(BF16)
    return w_main, w_gate


def _gate_rows(pg, lo, b, t):
    g = pg[:, lo:lo + N_HEADS].reshape(b, t, N_HEADS)
    return jnp.transpose(g, (0, 2, 1)).reshape(b, N_HEADS, t // CHUNK, CHUNK)


def kernel(x, norm_w, final_norm_w, gdn_w_in, gdn_conv_w, gdn_a_log, gdn_dt_bias, gdn_norm_w, gdn_w_out,
           mlstm_w_in, mlstm_conv_w, mlstm_i_bias, mlstm_f_bias, mlstm_norm_w, mlstm_w_out):
    b, t, d = x.shape
    depth = norm_w.shape[0]
    x2d = x.reshape(b * t, d)
    for i in range(depth):
        j = i // 2
        last = final_norm_w if i == depth - 1 else None
        if i % 2 == 0:
            w_main, w_gate = _split_in_weight(gdn_w_in[j])
            p, pg = _inproj(x2d, norm_w[i], w_main, w_gate)
            g = _gdn_recurrence(p.reshape(b, t, -1), gdn_conv_w[j], _gate_rows(pg, 0, b, t),
                                _gate_rows(pg, N_HEADS, b, t), gdn_a_log[j], gdn_dt_bias[j], gdn_norm_w[j])
            x2d = _outproj(g.reshape(b * t, D_INNER), gdn_w_out[j].astype(BF16), x2d, last)
        else:
            w_main, w_gate = _split_in_weight(mlstm_w_in[j])
            p, pg = _inproj(x2d, norm_w[i], w_main, w_gate)
            g = _mlstm_recurrence(p.reshape(b, t, -1), mlstm_conv_w[j], _gate_rows(pg, 0, b, t),
                                  _gate_rows(pg, N_HEADS, b, t), mlstm_i_bias[j], mlstm_f_bias[j],
                                  mlstm_norm_w[j])
            x2d = _outproj(g.reshape(b * t, D_INNER), mlstm_w_out[j].astype(BF16), x2d, last)
    return x2d.reshape(b, t, d)
```

```python
import jax
import jax.numpy as jnp
from jax import lax
from jax.experimental import pallas as pl
from jax.experimental.pallas import tpu as pltpu

F32 = jnp.float32
BF16 = jnp.bfloat16
HIGHEST = lax.Precision.HIGHEST

D_MODEL = 1024
N_HEADS = 8
HEAD_QK = 128
HEAD_V = 256
D_INNER = N_HEADS * HEAD_V
QK_W = N_HEADS * HEAD_QK
CONV_K = 4
CHUNK = 64
NORM_EPS = 1e-6
GATE_PAD = 128

INPROJ_ROW_TILE = 1024
ROW_TILE = 512
COL_TILE = 1024
TIME_BLOCK = 512
CHUNKS_PER_BLOCK = TIME_BLOCK // CHUNK
CONV_HALO = 8
VMEM_LIMIT_BYTES = 48 * 1024 * 1024


def _dot(a, b):
    return jnp.dot(a.astype(BF16), b.astype(BF16), preferred_element_type=F32)


def _dot_nt(a, b):
    return lax.dot_general(a.astype(BF16), b.astype(BF16), (((1,), (1,)), ((), ())),
                           preferred_element_type=F32)


def _dot_tn(a, b):
    return lax.dot_general(a.astype(BF16), b.astype(BF16), (((0,), (0,)), ((), ())),
                           preferred_element_type=F32)


def _dot_f32(a, b):
    return jnp.dot(a, b, precision=HIGHEST, preferred_element_type=F32)


def _sigmoid(x):
    return jax.nn.sigmoid(x)


def _silu(x):
    return x * _sigmoid(x)


def _softplus(x):
    return jnp.maximum(x, 0.0) + jnp.log1p(jnp.exp(-jnp.abs(x)))


def _rms(x, w):
    return x * lax.rsqrt(jnp.mean(x * x, axis=-1, keepdims=True) + NORM_EPS) * w


def _inproj_kernel(x_ref, nw_ref, w_ref, wg_ref, p_ref, pg_ref, hn_ref):
    @pl.when(pl.program_id(1) == 0)
    def _():
        hn = _rms(x_ref[...], nw_ref[...]).astype(BF16)
        hn_ref[...] = hn
        pg_ref[...] = jnp.dot(hn, wg_ref[...], preferred_element_type=F32)

    p_ref[...] = jnp.dot(hn_ref[...], w_ref[...], preferred_element_type=F32)


def _inproj(x2d, nw, w_main, w_gate):
    m, d = x2d.shape
    n = w_main.shape[1]
    return pl.pallas_call(
        _inproj_kernel,
        grid=(m // INPROJ_ROW_TILE, n // COL_TILE),
        in_specs=[
            pl.BlockSpec((INPROJ_ROW_TILE, d), lambda i, j: (i, 0)),
            pl.BlockSpec((1, d), lambda i, j: (0, 0)),
            pl.BlockSpec((d, COL_TILE), lambda i, j: (0, j)),
            pl.BlockSpec((d, GATE_PAD), lambda i, j: (0, 0)),
        ],
        out_specs=[
            pl.BlockSpec((INPROJ_ROW_TILE, COL_TILE), lambda i, j: (i, j)),
            pl.BlockSpec((INPROJ_ROW_TILE, GATE_PAD), lambda i, j: (i, 0)),
        ],
        out_shape=[jax.ShapeDtypeStruct((m, n), F32), jax.ShapeDtypeStruct((m, GATE_PAD), F32)],
        scratch_shapes=[pltpu.VMEM((INPROJ_ROW_TILE, d), BF16)],
        compiler_params=pltpu.CompilerParams(
            dimension_semantics=("parallel", "arbitrary"), vmem_limit_bytes=VMEM_LIMIT_BYTES),
        name="inproj",
    )(x2d, nw.reshape(1, d), w_main, w_gate)


def _outproj_kernel(g_ref, w_ref, x_ref, o_ref):
    o_ref[...] = x_ref[...] + jnp.dot(g_ref[...], w_ref[...], preferred_element_type=F32)


def _outproj_final_kernel(g_ref, w_ref, x_ref, fw_ref, o_ref):
    y = x_ref[...] + jnp.dot(g_ref[...], w_ref[...], preferred_element_type=F32)
    o_ref[...] = _rms(y, fw_ref[...])


def _outproj(g2d, w_out, x2d, final_w=None):
    m, d = x2d.shape
    k = g2d.shape[1]
    in_specs = [
        pl.BlockSpec((ROW_TILE, k), lambda i: (i, 0)),
        pl.BlockSpec((k, d), lambda i: (0, 0)),
        pl.BlockSpec((ROW_TILE, d), lambda i: (i, 0)),
    ]
    args = [g2d, w_out, x2d]
    body = _outproj_kernel
    if final_w is not None:
        in_specs.append(pl.BlockSpec((1, d), lambda i: (0, 0)))
        args.append(final_w.reshape(1, d))
        body = _outproj_final_kernel
    return pl.pallas_call(
        body,
        grid=(m // ROW_TILE,),
        in_specs=in_specs,
        out_specs=pl.BlockSpec((ROW_TILE, d), lambda i: (i, 0)),
        out_shape=jax.ShapeDtypeStruct((m, d), F32),
        compiler_params=pltpu.CompilerParams(
            dimension_semantics=("parallel",), vmem_limit_bytes=VMEM_LIMIT_BYTES),
        name="outproj_final" if final_w is not None else "outproj",
    )(*args)


def _causal_conv_silu(raw_ref, buf_ref, cw_ref):
    tb = TIME_BLOCK
    buf_ref[CONV_HALO:CONV_HALO + tb, :] = raw_ref[0]
    cw = cw_ref[...]
    acc = cw[0:1, :] * buf_ref[CONV_HALO - 3:CONV_HALO - 3 + tb, :]
    for j in range(1, CONV_K):
        lo = CONV_HALO - 3 + j
        acc = acc + cw[j:j + 1, :] * buf_ref[lo:lo + tb, :]
    buf_ref[0:CONV_HALO, :] = buf_ref[tb:tb + CONV_HALO, :]
    return _silu(acc)


def _chunk_iotas():
    r = lax.broadcasted_iota(jnp.int32, (CHUNK, CHUNK), 0)
    c = lax.broadcasted_iota(jnp.int32, (CHUNK, CHUNK), 1)
    return r, c


def _col(xt, n, width):
    return jnp.broadcast_to(xt[:, n:n + 1], (CHUNK, width))


def _columns(rows_list):
    rows = jnp.concatenate(rows_list, axis=0)
    rows = jnp.concatenate([rows, jnp.zeros((128 - rows.shape[0], CHUNK), F32)], axis=0)
    rows = jnp.concatenate([rows, jnp.zeros((128, 128 - CHUNK), F32)], axis=1)
    return rows.T[:CHUNK, :]


def _unit_lower_inverse_minus_identity(a_list, r, c):
    base = 8
    blk = (r // base) == (c // base)
    a0 = [jnp.where(blk, a, 0.0) for a in a_list]
    a2 = [_dot(x, x) for x in a0]
    a34 = [_dot(jnp.concatenate([x, y], axis=0), y) for x, y in zip(a0, a2)]
    p = [y - x - z[:CHUNK] for x, y, z in zip(a0, a2, a34)]
    a4 = [z[CHUNK:] for z in a34]
    n = [x + y + _dot(x, y) for x, y in zip(p, a4)]
    s = base
    while s < CHUNK:
        off_mask = ((r // (2 * s)) == (c // (2 * s))) & (((r // s) % 2) == 1) & (((c // s) % 2) == 0)
        off = [jnp.where(off_mask, a, 0.0) for a in a_list]
        x = [o + _dot(o, m) for o, m in zip(off, n)]
        n = [m - y - _dot(m, y) for m, y in zip(n, x)]
        s *= 2
    return n


GDN_HEADS_PER_STEP = 4
_GDN_COLUMN_KINDS = 5


def _gdn_kernel(alog_ref, dtb_ref, q_ref, k_ref, v_ref, z_ref, cwq_ref, cwk_ref, cwv_ref,
                a_ref, bt_ref, nw_ref, o_ref, qbuf, kbuf, vbuf, s_ref):
    hp, nc = GDN_HEADS_PER_STEP, CHUNKS_PER_BLOCK
    hg = pl.program_id(1)

    @pl.when(pl.program_id(2) == 0)
    def _():
        s_ref[...] = jnp.zeros_like(s_ref)
        qbuf[0:CONV_HALO, :] = jnp.zeros((CONV_HALO, hp * HEAD_QK), F32)
        kbuf[0:CONV_HALO, :] = jnp.zeros((CONV_HALO, hp * HEAD_QK), F32)
        vbuf[0:CONV_HALO, :] = jnp.zeros((CONV_HALO, hp * HEAD_V), F32)

    q = _causal_conv_silu(q_ref, qbuf, cwq_ref)
    k = _causal_conv_silu(k_ref, kbuf, cwk_ref)
    v = _causal_conv_silu(v_ref, vbuf, cwv_ref)

    r, c = _chunk_iotas()
    incl = r >= c
    strict = r > c
    upper = (r <= c).astype(F32)

    qh, kh, vh, gcs, es, rows = [], [], [], [], [], []
    for hd in range(hp):
        h = hg * hp + hd
        qs = q[:, hd * HEAD_QK:(hd + 1) * HEAD_QK]
        ks = k[:, hd * HEAD_QK:(hd + 1) * HEAD_QK]
        qh.append(qs * lax.rsqrt(jnp.sum(qs * qs, axis=-1, keepdims=True) + 1e-6) * (HEAD_QK ** -0.5))
        kh.append(ks * lax.rsqrt(jnp.sum(ks * ks, axis=-1, keepdims=True) + 1e-6))
        vh.append(v[:, hd * HEAD_V:(hd + 1) * HEAD_V])
        neg_rate = -jnp.exp(jnp.full((1, 1), alog_ref[h], F32))
        g = neg_rate * _softplus(a_ref[0, hd] + dtb_ref[h])
        beta = _sigmoid(bt_ref[0, hd])
        gc = _dot_f32(g, upper)
        e = jnp.exp(gc)
        gcs.append(gc)
        es.append(e)
        rows += [gc, beta, beta * e, e, jnp.exp(gc[:, CHUNK - 1:CHUNK] - gc)]
    heads_per_transpose = 128 // (_GDN_COLUMN_KINDS * nc)
    cols = [_columns(rows[i * _GDN_COLUMN_KINDS:(i + heads_per_transpose) * _GDN_COLUMN_KINDS])
            for i in range(0, hp, heads_per_transpose)]

    def col(hd, kind, n, width):
        i = ((hd % heads_per_transpose) * _GDN_COLUMN_KINDS + kind) * nc + n
        return _col(cols[hd // heads_per_transpose], i, width)

    chains = [(hd, n) for n in range(nc) for hd in range(hp)]
    q_c = {ch: qh[ch[0]][ch[1] * CHUNK:(ch[1] + 1) * CHUNK] for ch in chains}
    k_c = {ch: kh[ch[0]][ch[1] * CHUNK:(ch[1] + 1) * CHUNK] for ch in chains}
    v_c = {ch: vh[ch[0]][ch[1] * CHUNK:(ch[1] + 1) * CHUNK] for ch in chains}

    kq = {ch: _dot_nt(jnp.concatenate([k_c[ch], q_c[ch]], axis=0), k_c[ch]) for ch in chains}
    a_list, attn = [], {}
    for ch in chains:
        hd, n = ch
        gc_row = gcs[hd][n:n + 1, :]
        decay = jnp.where(incl, jnp.exp(jnp.where(incl, col(hd, 0, n, CHUNK) - gc_row, 0.0)), 0.0)
        a_list.append(jnp.where(strict, col(hd, 1, n, CHUNK) * kq[ch][:CHUNK] * decay, 0.0))
        attn[ch] = (kq[ch][CHUNK:] * decay).astype(BF16)
    n_list = _unit_lower_inverse_minus_identity(a_list, r, c)
    uw, kd = {}, {}
    for ch, nmat in zip(chains, n_list):
        hd, n = ch
        rhs = jnp.concatenate([col(hd, 1, n, HEAD_V) * v_c[ch], col(hd, 2, n, HEAD_QK) * k_c[ch]], axis=1)
        uw[ch] = (rhs + _dot(nmat, rhs)).astype(BF16)
        kd[ch] = (k_c[ch] * col(hd, 4, n, HEAD_QK)).astype(BF16)
    kd_uw = {ch: _dot_tn(kd[ch], uw[ch]) for ch in chains}
    at_uw = {ch: _dot(attn[ch], uw[ch]) for ch in chains}
    lhs = {}
    for ch in chains:
        hd, n = ch
        to_out = q_c[ch] * col(hd, 3, n, HEAD_QK) - at_uw[ch][:, HEAD_V:]
        lhs[ch] = jnp.concatenate([-kd_uw[ch][:, HEAD_V:], to_out], axis=0).astype(BF16)

    nw = nw_ref[...]
    s = [s_ref[hd] for hd in range(hp)]
    for n in range(nc):
        lo = n * CHUNK
        m = [_dot(lhs[(hd, n)], s[hd]) for hd in range(hp)]
        o = [m[hd][HEAD_QK:] + at_uw[(hd, n)][:, :HEAD_V] for hd in range(hp)]
        s = [es[hd][n:n + 1, CHUNK - 1:CHUNK] * s[hd] + m[hd][:HEAD_QK] + kd_uw[(hd, n)][:, :HEAD_V]
             for hd in range(hp)]
        for hd in range(hp):
            z = z_ref[0, lo:lo + CHUNK, hd * HEAD_V:(hd + 1) * HEAD_V]
            o_ref[0, lo:lo + CHUNK, hd * HEAD_V:(hd + 1) * HEAD_V] = (_rms(o[hd], nw) * _silu(z)).astype(o_ref.dtype)
    for hd in range(hp):
        s_ref[hd] = s[hd]


def _gdn_recurrence(p3, conv_w, a_r, bt_r, a_log, dt_bias, norm_w):
    b, t, _ = p3.shape
    tb, hp = TIME_BLOCK, GDN_HEADS_PER_STEP
    wq, wv = hp * HEAD_QK, hp * HEAD_V
    nk = QK_W // wq
    nv = 2 * QK_W // wv
    nz = (2 * QK_W + D_INNER) // wv
    smem = pl.BlockSpec(memory_space=pltpu.SMEM)
    return pl.pallas_call(
        _gdn_kernel,
        grid=(b, N_HEADS // hp, t // tb),
        in_specs=[
            smem, smem,
            pl.BlockSpec((1, tb, wq), lambda bi, h, ti: (bi, ti, h)),
            pl.BlockSpec((1, tb, wq), lambda bi, h, ti: (bi, ti, nk + h)),
            pl.BlockSpec((1, tb, wv), lambda bi, h, ti: (bi, ti, nv + h)),
            pl.BlockSpec((1, tb, wv), lambda bi, h, ti: (bi, ti, nz + h)),
            pl.BlockSpec((CONV_K, wq), lambda bi, h, ti: (0, h)),
            pl.BlockSpec((CONV_K, wq), lambda bi, h, ti: (0, nk + h)),
            pl.BlockSpec((CONV_K, wv), lambda bi, h, ti: (0, nv + h)),
            pl.BlockSpec((1, hp, CHUNKS_PER_BLOCK, CHUNK), lambda bi, h, ti: (bi, h, ti, 0)),
            pl.BlockSpec((1, hp, CHUNKS_PER_BLOCK, CHUNK), lambda bi, h, ti: (bi, h, ti, 0)),
            pl.BlockSpec((1, HEAD_V), lambda bi, h, ti: (0, 0)),
        ],
        out_specs=pl.BlockSpec((1, tb, wv), lambda bi, h, ti: (bi, ti, h)),
        out_shape=jax.ShapeDtypeStruct((b, t, D_INNER), BF16),
        scratch_shapes=[
            pltpu.VMEM((tb + CONV_HALO, wq), F32),
            pltpu.VMEM((tb + CONV_HALO, wq), F32),
            pltpu.VMEM((tb + CONV_HALO, wv), F32),
            pltpu.VMEM((hp, HEAD_QK, HEAD_V), F32),
        ],
        compiler_params=pltpu.CompilerParams(
            dimension_semantics=("parallel", "parallel", "arbitrary"),
            vmem_limit_bytes=VMEM_LIMIT_BYTES),
        name="gdn_recurrence",
    )(a_log, dt_bias, p3, p3, p3, p3, conv_w, conv_w, conv_w, a_r, bt_r, norm_w.reshape(1, HEAD_V))


MLSTM_HEADS_PER_STEP = 2
_MLSTM_COLUMN_KINDS = 4


def _prefix_max_lanes(x):
    y = jnp.concatenate([x, jnp.full((x.shape[0], 128 - CHUNK), -jnp.inf, F32)], axis=1)
    s = 1
    while s < CHUNK:
        y = jnp.maximum(y, pltpu.roll(y, s, axis=1))
        s *= 2
    return y[:, :CHUNK]


def _mlstm_kernel(ib_ref, fb_ref, q_ref, k_ref, v_ref, og_ref, z_ref, cwq_ref, cwk_ref,
                  ig_ref, fg_ref, nw_ref, o_ref, qbuf, kbuf, s_ref, m_ref):
    hp, nc = MLSTM_HEADS_PER_STEP, CHUNKS_PER_BLOCK
    hg = pl.program_id(1)

    @pl.when(pl.program_id(2) == 0)
    def _():
        s_ref[...] = jnp.zeros_like(s_ref)
        m_ref[...] = jnp.zeros_like(m_ref)
        qbuf[0:CONV_HALO, :] = jnp.zeros((CONV_HALO, hp * HEAD_QK), F32)
        kbuf[0:CONV_HALO, :] = jnp.zeros((CONV_HALO, hp * HEAD_QK), F32)

    q = _causal_conv_silu(q_ref, qbuf, cwq_ref)
    k = _causal_conv_silu(k_ref, kbuf, cwk_ref) * (HEAD_QK ** -0.5)

    r, c = _chunk_iotas()
    incl = r >= c
    upper = (r <= c).astype(F32)
    ones_col = (lax.broadcasted_iota(jnp.int32, (CHUNK, 128), 1) == 0).astype(F32)
    chunk_id = lax.broadcasted_iota(jnp.int32, (nc, 1), 0)

    rows, d_rows, w_state, m_out = [], [], [], []
    for hd in range(hp):
        h = hg * hp + hd
        i_pre = ig_ref[0, hd] + ib_ref[h]
        lf = -_softplus(-(fg_ref[0, hd] + fb_ref[h]))
        bc = _dot_f32(lf, upper)
        b_last = bc[:, CHUNK - 1:CHUNK]
        d = i_pre - bc
        g_end = b_last + d
        g_max = jnp.max(g_end, axis=-1, keepdims=True)
        r_max = bc + _prefix_max_lanes(d)
        m = m_ref[hd, 0:1, 0:1]
        m_before = jnp.zeros((nc, 1), F32)
        m_after = jnp.zeros((nc, 1), F32)
        for n in range(nc):
            m_before = jnp.where(chunk_id == n, m, m_before)
            m = jnp.maximum(b_last[n:n + 1, :] + m, g_max[n:n + 1, :])
            m_after = jnp.where(chunk_id == n, m, m_after)
        m_out.append(m)
        m_t = jnp.maximum(bc + m_before, r_max)
        d_rows.append(d)
        w_state.append(jnp.exp(b_last + m_before - m_after))
        rows += [bc - m_t, jnp.exp(bc + m_before - m_t), jnp.exp(-m_t), jnp.exp(g_end - m_after)]
    cols = _columns(rows)

    def col(hd, kind, n, width):
        return _col(cols, (hd * _MLSTM_COLUMN_KINDS + kind) * nc + n, width)

    chains = [(hd, n) for n in range(nc) for hd in range(hp)]
    q_c = {ch: q[ch[1] * CHUNK:(ch[1] + 1) * CHUNK, ch[0] * HEAD_QK:(ch[0] + 1) * HEAD_QK] for ch in chains}
    k_c = {ch: k[ch[1] * CHUNK:(ch[1] + 1) * CHUNK, ch[0] * HEAD_QK:(ch[0] + 1) * HEAD_QK] for ch in chains}
    v_ext = {ch: jnp.concatenate([v_ref[0, ch[1] * CHUNK:(ch[1] + 1) * CHUNK, ch[0] * HEAD_V:(ch[0] + 1) * HEAD_V],
                                  ones_col], axis=1).astype(BF16) for ch in chains}

    qk = {ch: _dot_nt(q_c[ch], k_c[ch]) for ch in chains}
    amat, kw = {}, {}
    for ch in chains:
        hd, n = ch
        arg = col(hd, 0, n, CHUNK) + d_rows[hd][n:n + 1, :]
        amat[ch] = (qk[ch] * jnp.exp(jnp.where(incl, arg, -jnp.inf))).astype(BF16)
        kw[ch] = (k_c[ch] * col(hd, 3, n, HEAD_QK)).astype(BF16)
    av = {ch: _dot(amat[ch], v_ext[ch]) for ch in chains}
    kv = {ch: _dot_tn(kw[ch], v_ext[ch]) for ch in chains}

    s_in = {}
    for hd in range(hp):
        s = s_ref[hd]
        for n in range(nc):
            s_in[(hd, n)] = s
            s = w_state[hd][n:n + 1, :] * s + kv[(hd, n)]
        s_ref[hd] = s
        m_ref[hd] = jnp.broadcast_to(m_out[hd], m_ref.shape[1:])
    qs = {ch: _dot(q_c[ch], s_in[ch]) for ch in chains}

    nw = nw_ref[...]
    for ch in chains:
        hd, n = ch
        lo = n * CHUNK
        num_ext = col(hd, 1, n, HEAD_V + 128) * qs[ch] + av[ch]
        denom = jnp.maximum(jnp.abs(num_ext[:, HEAD_V:HEAD_V + 1]), col(hd, 2, n, 1))
        hh = num_ext[:, :HEAD_V] / denom
        gate = (_sigmoid(og_ref[0, lo:lo + CHUNK, hd * HEAD_V:(hd + 1) * HEAD_V])
                * _silu(z_ref[0, lo:lo + CHUNK, hd * HEAD_V:(hd + 1) * HEAD_V]))
        o_ref[0, lo:lo + CHUNK, hd * HEAD_V:(hd + 1) * HEAD_V] = (_rms(hh, nw) * gate).astype(o_ref.dtype)


def _mlstm_recurrence(p3, conv_w, ig_r, fg_r, i_bias, f_bias, norm_w):
    b, t, _ = p3.shape
    tb, hp = TIME_BLOCK, MLSTM_HEADS_PER_STEP
    wq, wv = hp * HEAD_QK, hp * HEAD_V
    nk = QK_W // wq
    nv = 2 * QK_W // wv
    nh = D_INNER // wv
    smem = pl.BlockSpec(memory_space=pltpu.SMEM)
    return pl.pallas_call(
        _mlstm_kernel,
        grid=(b, N_HEADS // hp, t // tb),
        in_specs=[
            smem, smem,
            pl.BlockSpec((1, tb, wq), lambda bi, h, ti: (bi, ti, h)),
            pl.BlockSpec((1, tb, wq), lambda bi, h, ti: (bi, ti, nk + h)),
            pl.BlockSpec((1, tb, wv), lambda bi, h, ti: (bi, ti, nv + h)),
            pl.BlockSpec((1, tb, wv), lambda bi, h, ti: (bi, ti, nv + nh + h)),
            pl.BlockSpec((1, tb, wv), lambda bi, h, ti: (bi, ti, nv + 2 * nh + h)),
            pl.BlockSpec((CONV_K, wq), lambda bi, h, ti: (0, h)),
            pl.BlockSpec((CONV_K, wq), lambda bi, h, ti: (0, nk + h)),
            pl.BlockSpec((1, hp, CHUNKS_PER_BLOCK, CHUNK), lambda bi, h, ti: (bi, h, ti, 0)),
            pl.BlockSpec((1, hp, CHUNKS_PER_BLOCK, CHUNK), lambda bi, h, ti: (bi, h, ti, 0)),
            pl.BlockSpec((1, HEAD_V), lambda bi, h, ti: (0, 0)),
        ],
        out_specs=pl.BlockSpec((1, tb, wv), lambda bi, h, ti: (bi, ti, h)),
        out_shape=jax.ShapeDtypeStruct((b, t, D_INNER), BF16),
        scratch_shapes=[
            pltpu.VMEM((tb + CONV_HALO, wq), F32),
            pltpu.VMEM((tb + CONV_HALO, wq), F32),
            pltpu.VMEM((hp, HEAD_QK, HEAD_V + 128), F32),
            pltpu.VMEM((hp, 8, 128), F32),
        ],
        compiler_params=pltpu.CompilerParams(
            dimension_semantics=("parallel", "parallel", "arbitrary"),
            vmem_limit_bytes=VMEM_LIMIT_BYTES),
        name="mlstm_recurrence",
    )(i_bias, f_bias, p3, p3, p3, p3, p3, conv_w, conv_w, ig_r, fg_r, norm_w.reshape(1, HEAD_V))


def _split_in_weight(w_in):
    n_main = w_in.shape[1] - 2 * N_HEADS
    w_main = w_in[:, :n_main].astype(BF16)
    w_gate = jnp.pad(w_in[:, n_main:], ((0, 0), (0, GATE_PAD - 2 * N_HEADS))).astype(BF16)
    return w_main, w_gate


def _gate_rows(pg, lo, b, t):
    g = pg[:, lo:lo + N_HEADS].reshape(b, t, N_HEADS)
    return jnp.transpose(g, (0, 2, 1)).reshape(b, N_HEADS, t // CHUNK, CHUNK)


def kernel(x, norm_w, final_norm_w, gdn_w_in, gdn_conv_w, gdn_a_log, gdn_dt_bias, gdn_norm_w, gdn_w_out,
           mlstm_w_in, mlstm_conv_w, mlstm_i_bias, mlstm_f_bias, mlstm_norm_w, mlstm_w_out):
    b, t, d = x.shape
    depth = norm_w.shape[0]
    x2d = x.reshape(b * t, d)
    for i in range(depth):
        j = i // 2
        last = final_norm_w if i == depth - 1 else None
        if i % 2 == 0:
            w_main, w_gate = _split_in_weight(gdn_w_in[j])
            p, pg = _inproj(x2d, norm_w[i], w_main, w_gate)
            g = _gdn_recurrence(p.reshape(b, t, -1), gdn_conv_w[j], _gate_rows(pg, 0, b, t),
                                _gate_rows(pg, N_HEADS, b, t), gdn_a_log[j], gdn_dt_bias[j], gdn_norm_w[j])
            x2d = _outproj(g.reshape(b * t, D_INNER), gdn_w_out[j].astype(BF16), x2d, last)
        else:
            w_main, w_gate = _split_in_weight(mlstm_w_in[j])
            p, pg = _inproj(x2d, norm_w[i], w_main, w_gate)
            g = _mlstm_recurrence(p.reshape(b, t, -1), mlstm_conv_w[j], _gate_rows(pg, 0, b, t),
                                  _gate_rows(pg, N_HEADS, b, t), mlstm_i_bias[j], mlstm_f_bias[j],
                                  mlstm_norm_w[j])
            x2d = _outproj(g.reshape(b * t, D_INNER), mlstm_w_out[j].astype(BF16), x2d, last)
    return x2d.reshape(b, t, d)
```

```python
import functools

import jax
import jax.numpy as jnp
from jax import lax
from jax.experimental import pallas as pl
from jax.experimental.pallas import tpu as pltpu

F32 = jnp.float32
BF16 = jnp.bfloat16
HIGHEST = lax.Precision.HIGHEST

D_MODEL = 1024
N_HEADS = 8
HEAD_QK = 128
HEAD_V = 256
D_INNER = N_HEADS * HEAD_V
QK_W = N_HEADS * HEAD_QK
CONV_K = 4
CHUNK = 64
NORM_EPS = 1e-6
GATE_PAD = 128

INPROJ_ROW_TILE = 1024
INPROJ_SUB_ROWS = 128
ROW_TILE = 512
OUTPROJ_SUB_ROWS = 256
COL_TILE = 1024
TIME_BLOCK = 512
CHUNKS_PER_BLOCK = TIME_BLOCK // CHUNK
CONV_HALO = 8
VMEM_LIMIT_BYTES = 48 * 1024 * 1024


def _dot(a, b):
    return jnp.dot(a.astype(BF16), b.astype(BF16), preferred_element_type=F32)


def _dot_nt(a, b):
    return lax.dot_general(a.astype(BF16), b.astype(BF16), (((1,), (1,)), ((), ())),
                           preferred_element_type=F32)


def _dot_tn(a, b):
    return lax.dot_general(a.astype(BF16), b.astype(BF16), (((0,), (0,)), ((), ())),
                           preferred_element_type=F32)


def _dot_f32(a, b):
    return jnp.dot(a, b, precision=HIGHEST, preferred_element_type=F32)


def _sigmoid(x):
    return jax.nn.sigmoid(x)


def _silu(x):
    return x * _sigmoid(x)


def _softplus(x):
    return jnp.maximum(x, 0.0) + jnp.log1p(jnp.exp(-jnp.abs(x)))


def _rms(x, w):
    return x * lax.rsqrt(jnp.mean(x * x, axis=-1, keepdims=True) + NORM_EPS) * w


def _prenorm_kernel(x_ref, nw_ref, wg_ref, h_ref, pg_ref):
    h = _rms(x_ref[...], nw_ref[...]).astype(BF16)
    h_ref[...] = h
    pg_ref[...] = jnp.dot(h, wg_ref[...], preferred_element_type=F32)


def _prenorm(x2d, nw, w_gate):
    m, d = x2d.shape
    return pl.pallas_call(
        _prenorm_kernel,
        grid=(m // ROW_TILE,),
        in_specs=[
            pl.BlockSpec((ROW_TILE, d), lambda i: (i, 0)),
            pl.BlockSpec((1, d), lambda i: (0, 0)),
            pl.BlockSpec((d, GATE_PAD), lambda i: (0, 0)),
        ],
        out_specs=[
            pl.BlockSpec((ROW_TILE, d), lambda i: (i, 0)),
            pl.BlockSpec((ROW_TILE, GATE_PAD), lambda i: (i, 0)),
        ],
        out_shape=[jax.ShapeDtypeStruct((m, d), BF16), jax.ShapeDtypeStruct((m, GATE_PAD), F32)],
        compiler_params=pltpu.CompilerParams(
            dimension_semantics=("parallel",), vmem_limit_bytes=VMEM_LIMIT_BYTES),
        name="prenorm",
    )(x2d, nw.reshape(1, d), w_gate)


def _conv_taps(tail, y, cw):
    rows = y.shape[0]
    ext = jnp.concatenate([tail, y], axis=0)
    acc = cw[CONV_K - 1:CONV_K, :] * y
    for j in range(CONV_K - 1):
        lo = CONV_HALO - (CONV_K - 1) + j
        acc = acc + cw[j:j + 1, :] * ext[lo:lo + rows, :]
    return acc


_ACTIVATIONS = {"none": lambda y: y, "silu": _silu, "sigmoid": _sigmoid}


def _inproj_kernel(h_ref, w_ref, cw_ref, p_ref, wb_ref, tail_ref, *, tile_modes, row_tiles_per_seq):
    j, i = pl.program_id(0), pl.program_id(1)

    @pl.when(i == 0)
    def _():
        wb_ref[...] = w_ref[...].astype(BF16)

    def matmul(r0):
        return jnp.dot(h_ref[r0:r0 + INPROJ_SUB_ROWS, :], wb_ref[...], preferred_element_type=F32)

    for lo, hi, mode in tile_modes:
        @pl.when((j >= lo) & (j < hi))
        def _(mode=mode):
            if mode != "conv_silu":
                for r0 in range(0, INPROJ_ROW_TILE, INPROJ_SUB_ROWS):
                    p_ref[r0:r0 + INPROJ_SUB_ROWS, :] = _ACTIVATIONS[mode](matmul(r0)).astype(p_ref.dtype)
                return

            @pl.when(i % row_tiles_per_seq == 0)
            def _():
                tail_ref[...] = jnp.zeros_like(tail_ref)

            cw = cw_ref[...]
            tail = tail_ref[...]
            for r0 in range(0, INPROJ_ROW_TILE, INPROJ_SUB_ROWS):
                y = matmul(r0)
                p_ref[r0:r0 + INPROJ_SUB_ROWS, :] = _silu(_conv_taps(tail, y, cw)).astype(p_ref.dtype)
                tail = y[INPROJ_SUB_ROWS - CONV_HALO:, :]
            tail_ref[...] = tail


def _inproj(h2d, w_in, layer, conv_w, tile_modes, seq_len):
    m, d = h2d.shape
    n_main = tile_modes[-1][1] * COL_TILE
    n_conv = conv_w.shape[1] // COL_TILE
    body = functools.partial(_inproj_kernel, tile_modes=tile_modes, row_tiles_per_seq=seq_len // INPROJ_ROW_TILE)
    return pl.pallas_call(
        body,
        grid=(n_main // COL_TILE, m // INPROJ_ROW_TILE),
        in_specs=[
            pl.BlockSpec((INPROJ_ROW_TILE, d), lambda j, i: (i, 0)),
            pl.BlockSpec((None, d, COL_TILE), lambda j, i: (layer, 0, j)),
            pl.BlockSpec((CONV_K, COL_TILE), lambda j, i: (0, jnp.minimum(j, n_conv - 1))),
        ],
        out_specs=pl.BlockSpec((INPROJ_ROW_TILE, COL_TILE), lambda j, i: (i, j)),
        out_shape=jax.ShapeDtypeStruct((m, n_main), BF16),
        scratch_shapes=[pltpu.VMEM((d, COL_TILE), BF16),
                        pltpu.VMEM((CONV_HALO, COL_TILE), F32)],
        compiler_params=pltpu.CompilerParams(
            dimension_semantics=("arbitrary", "arbitrary"), vmem_limit_bytes=VMEM_LIMIT_BYTES),
        name="inproj",
    )(h2d, w_in, conv_w)


def _outproj_kernel(g_ref, w_ref, x_ref, nw_ref, wg_ref, xo_ref, h_ref, pg_ref, wb_ref):
    @pl.when(pl.program_id(0) == 0)
    def _():
        wb_ref[...] = w_ref[...].astype(BF16)

    for r0 in range(0, ROW_TILE, OUTPROJ_SUB_ROWS):
        rows = slice(r0, r0 + OUTPROJ_SUB_ROWS)
        y = x_ref[rows, :] + jnp.dot(g_ref[rows, :], wb_ref[...], preferred_element_type=F32)
        xo_ref[rows, :] = y
        h = _rms(y, nw_ref[...]).astype(BF16)
        h_ref[rows, :] = h
        pg_ref[rows, :] = jnp.dot(h, wg_ref[...], preferred_element_type=F32)


def _outproj_final_kernel(g_ref, w_ref, x_ref, fw_ref, o_ref, wb_ref):
    @pl.when(pl.program_id(0) == 0)
    def _():
        wb_ref[...] = w_ref[...].astype(BF16)

    for r0 in range(0, ROW_TILE, OUTPROJ_SUB_ROWS):
        rows = slice(r0, r0 + OUTPROJ_SUB_ROWS)
        y = x_ref[rows, :] + jnp.dot(g_ref[rows, :], wb_ref[...], preferred_element_type=F32)
        o_ref[rows, :] = _rms(y, fw_ref[...])


def _outproj(g2d, w_out, layer, x2d, nw_next, w_gate_next):
    m, d = x2d.shape
    k = g2d.shape[1]
    row = lambda i: (i, 0)
    fixed = lambda i: (0, 0)
    return pl.pallas_call(
        _outproj_kernel,
        grid=(m // ROW_TILE,),
        in_specs=[
            pl.BlockSpec((ROW_TILE, k), row),
            pl.BlockSpec((None, k, d), lambda i: (layer, 0, 0)),
            pl.BlockSpec((ROW_TILE, d), row),
            pl.BlockSpec((1, d), fixed),
            pl.BlockSpec((d, GATE_PAD), fixed),
        ],
        out_specs=[
            pl.BlockSpec((ROW_TILE, d), row),
            pl.BlockSpec((ROW_TILE, d), row),
            pl.BlockSpec((ROW_TILE, GATE_PAD), row),
        ],
        out_shape=[jax.ShapeDtypeStruct((m, d), F32), jax.ShapeDtypeStruct((m, d), BF16),
                   jax.ShapeDtypeStruct((m, GATE_PAD), F32)],
        scratch_shapes=[pltpu.VMEM((k, d), BF16)],
        compiler_params=pltpu.CompilerParams(
            dimension_semantics=("arbitrary",), vmem_limit_bytes=VMEM_LIMIT_BYTES),
        name="outproj",
    )(g2d, w_out, x2d, nw_next.reshape(1, d), w_gate_next)


def _outproj_final(g2d, w_out, layer, x2d, final_w):
    m, d = x2d.shape
    k = g2d.shape[1]
    row = lambda i: (i, 0)
    return pl.pallas_call(
        _outproj_final_kernel,
        grid=(m // ROW_TILE,),
        in_specs=[
            pl.BlockSpec((ROW_TILE, k), row),
            pl.BlockSpec((None, k, d), lambda i: (layer, 0, 0)),
            pl.BlockSpec((ROW_TILE, d), row),
            pl.BlockSpec((1, d), lambda i: (0, 0)),
        ],
        out_specs=pl.BlockSpec((ROW_TILE, d), row),
        out_shape=jax.ShapeDtypeStruct((m, d), F32),
        scratch_shapes=[pltpu.VMEM((k, d), BF16)],
        compiler_params=pltpu.CompilerParams(
            dimension_semantics=("arbitrary",), vmem_limit_bytes=VMEM_LIMIT_BYTES),
        name="outproj_final",
    )(g2d, w_out, x2d, final_w.reshape(1, d))


def _chunk_iotas():
    r = lax.broadcasted_iota(jnp.int32, (CHUNK, CHUNK), 0)
    c = lax.broadcasted_iota(jnp.int32, (CHUNK, CHUNK), 1)
    return r, c


def _col(xt, n, width):
    return jnp.broadcast_to(xt[:, n:n + 1], (CHUNK, width))


def _columns(rows_list):
    rows = jnp.concatenate(rows_list, axis=0)
    rows = jnp.concatenate([rows, jnp.zeros((128 - rows.shape[0], CHUNK), F32)], axis=0)
    rows = jnp.concatenate([rows, jnp.zeros((128, 128 - CHUNK), F32)], axis=1)
    return rows.T[:CHUNK, :]


def _unit_lower_inverse_minus_identity(a_list, r, c):
    base = 8
    blk = (r // base) == (c // base)
    a0 = [jnp.where(blk, a, 0.0) for a in a_list]
    a2 = [_dot(x, x) for x in a0]
    a34 = [_dot(jnp.concatenate([x, y], axis=0), y) for x, y in zip(a0, a2)]
    p = [y - x - z[:CHUNK] for x, y, z in zip(a0, a2, a34)]
    a4 = [z[CHUNK:] for z in a34]
    n = [x + y + _dot(x, y) for x, y in zip(p, a4)]
    s = base
    while s < CHUNK:
        off_mask = ((r // (2 * s)) == (c // (2 * s))) & (((r // s) % 2) == 1) & (((c // s) % 2) == 0)
        off = [jnp.where(off_mask, a, 0.0) for a in a_list]
        x = [o + _dot(o, m) for o, m in zip(off, n)]
        n = [m - y - _dot(m, y) for m, y in zip(n, x)]
        s *= 2
    return n


GDN_HEADS_PER_STEP = 4
_GDN_COLUMN_KINDS = 5


def _gdn_kernel(alog_ref, dtb_ref, q_ref, k_ref, v_ref, z_ref, a_ref, bt_ref, nw_ref, o_ref, s_ref):
    hp, nc = GDN_HEADS_PER_STEP, CHUNKS_PER_BLOCK
    hg = pl.program_id(1)

    @pl.when(pl.program_id(2) == 0)
    def _():
        s_ref[...] = jnp.zeros_like(s_ref)

    q = q_ref[0].astype(F32)
    k = k_ref[0].astype(F32)
    v = v_ref[0].astype(F32)

    r, c = _chunk_iotas()
    incl = r >= c
    strict = r > c
    upper = (r <= c).astype(F32)

    qh, kh, vh, gcs, es, rows = [], [], [], [], [], []
    for hd in range(hp):
        h = hg * hp + hd
        qs = q[:, hd * HEAD_QK:(hd + 1) * HEAD_QK]
        ks = k[:, hd * HEAD_QK:(hd + 1) * HEAD_QK]
        qh.append(qs * lax.rsqrt(jnp.sum(qs * qs, axis=-1, keepdims=True) + 1e-6) * (HEAD_QK ** -0.5))
        kh.append(ks * lax.rsqrt(jnp.sum(ks * ks, axis=-1, keepdims=True) + 1e-6))
        vh.append(v[:, hd * HEAD_V:(hd + 1) * HEAD_V])
        neg_rate = -jnp.exp(jnp.full((1, 1), alog_ref[h], F32))
        g = neg_rate * _softplus(a_ref[0, hd] + dtb_ref[h])
        beta = _sigmoid(bt_ref[0, hd])
        gc = _dot_f32(g, upper)
        e = jnp.exp(gc)
        gcs.append(gc)
        es.append(e)
        rows += [gc, beta, beta * e, e, jnp.exp(gc[:, CHUNK - 1:CHUNK] - gc)]
    heads_per_transpose = 128 // (_GDN_COLUMN_KINDS * nc)
    cols = [_columns(rows[i * _GDN_COLUMN_KINDS:(i + heads_per_transpose) * _GDN_COLUMN_KINDS])
            for i in range(0, hp, heads_per_transpose)]

    def col(hd, kind, n, width):
        i = ((hd % heads_per_transpose) * _GDN_COLUMN_KINDS + kind) * nc + n
        return _col(cols[hd // heads_per_transpose], i, width)

    chains = [(hd, n) for n in range(nc) for hd in range(hp)]
    q_c = {ch: qh[ch[0]][ch[1] * CHUNK:(ch[1] + 1) * CHUNK] for ch in chains}
    k_c = {ch: kh[ch[0]][ch[1] * CHUNK:(ch[1] + 1) * CHUNK] for ch in chains}
    v_c = {ch: vh[ch[0]][ch[1] * CHUNK:(ch[1] + 1) * CHUNK] for ch in chains}

    kq = {ch: _dot_nt(jnp.concatenate([k_c[ch], q_c[ch]], axis=0), k_c[ch]) for ch in chains}
    a_list, attn = [], {}
    for ch in chains:
        hd, n = ch
        gc_row = gcs[hd][n:n + 1, :]
        decay = jnp.where(incl, jnp.exp(jnp.where(incl, col(hd, 0, n, CHUNK) - gc_row, 0.0)), 0.0)
        a_list.append(jnp.where(strict, col(hd, 1, n, CHUNK) * kq[ch][:CHUNK] * decay, 0.0))
        attn[ch] = (kq[ch][CHUNK:] * decay).astype(BF16)
    n_list = _unit_lower_inverse_minus_identity(a_list, r, c)
    uw, kd = {}, {}
    for ch, nmat in zip(chains, n_list):
        hd, n = ch
        rhs = jnp.concatenate([col(hd, 1, n, HEAD_V) * v_c[ch], col(hd, 2, n, HEAD_QK) * k_c[ch]], axis=1)
        uw[ch] = (rhs + _dot(nmat, rhs)).astype(BF16)
        kd[ch] = (k_c[ch] * col(hd, 4, n, HEAD_QK)).astype(BF16)
    kd_uw = {ch: _dot_tn(kd[ch], uw[ch]) for ch in chains}
    at_uw = {ch: _dot(attn[ch], uw[ch]) for ch in chains}
    lhs = {}
    for ch in chains:
        hd, n = ch
        to_out = q_c[ch] * col(hd, 3, n, HEAD_QK) - at_uw[ch][:, HEAD_V:]
        lhs[ch] = jnp.concatenate([-kd_uw[ch][:, HEAD_V:], to_out], axis=0).astype(BF16)

    nw = nw_ref[...]
    s = [s_ref[hd] for hd in range(hp)]
    for n in range(nc):
        lo = n * CHUNK
        m = [_dot(lhs[(hd, n)], s[hd]) for hd in range(hp)]
        o = [m[hd][HEAD_QK:] + at_uw[(hd, n)][:, :HEAD_V] for hd in range(hp)]
        s = [es[hd][n:n + 1, CHUNK - 1:CHUNK] * s[hd] + m[hd][:HEAD_QK] + kd_uw[(hd, n)][:, :HEAD_V]
             for hd in range(hp)]
        for hd in range(hp):
            z = z_ref[0, lo:lo + CHUNK, hd * HEAD_V:(hd + 1) * HEAD_V]
            o_ref[0, lo:lo + CHUNK, hd * HEAD_V:(hd + 1) * HEAD_V] = (_rms(o[hd], nw) * z.astype(F32)).astype(o_ref.dtype)
    for hd in range(hp):
        s_ref[hd] = s[hd]


def _gdn_recurrence(p3, a_r, bt_r, a_log, dt_bias, norm_w):
    b, t, _ = p3.shape
    tb, hp = TIME_BLOCK, GDN_HEADS_PER_STEP
    wq, wv = hp * HEAD_QK, hp * HEAD_V
    nk = QK_W // wq
    nv = 2 * QK_W // wv
    nz = (2 * QK_W + D_INNER) // wv
    smem = pl.BlockSpec(memory_space=pltpu.SMEM)
    return pl.pallas_call(
        _gdn_kernel,
        grid=(b, N_HEADS // hp, t // tb),
        in_specs=[
            smem, smem,
            pl.BlockSpec((1, tb, wq), lambda bi, h, ti: (bi, ti, h)),
            pl.BlockSpec((1, tb, wq), lambda bi, h, ti: (bi, ti, nk + h)),
            pl.BlockSpec((1, tb, wv), lambda bi, h, ti: (bi, ti, nv + h)),
            pl.BlockSpec((1, tb, wv), lambda bi, h, ti: (bi, ti, nz + h)),
            pl.BlockSpec((1, hp, CHUNKS_PER_BLOCK, CHUNK), lambda bi, h, ti: (bi, h, ti, 0)),
            pl.BlockSpec((1, hp, CHUNKS_PER_BLOCK, CHUNK), lambda bi, h, ti: (bi, h, ti, 0)),
            pl.BlockSpec((1, HEAD_V), lambda bi, h, ti: (0, 0)),
        ],
        out_specs=pl.BlockSpec((1, tb, wv), lambda bi, h, ti: (bi, ti, h)),
        out_shape=jax.ShapeDtypeStruct((b, t, D_INNER), BF16),
        scratch_shapes=[pltpu.VMEM((hp, HEAD_QK, HEAD_V), F32)],
        compiler_params=pltpu.CompilerParams(
            dimension_semantics=("parallel", "parallel", "arbitrary"),
            vmem_limit_bytes=VMEM_LIMIT_BYTES),
        name="gdn_recurrence",
    )(a_log, dt_bias, p3, p3, p3, p3, a_r, bt_r, norm_w.reshape(1, HEAD_V))


MLSTM_HEADS_PER_STEP = 2
_MLSTM_COLUMN_KINDS = 4


def _prefix_max_lanes(x):
    y = jnp.concatenate([x, jnp.full((x.shape[0], 128 - CHUNK), -jnp.inf, F32)], axis=1)
    s = 1
    while s < CHUNK:
        y = jnp.maximum(y, pltpu.roll(y, s, axis=1))
        s *= 2
    return y[:, :CHUNK]


def _mlstm_kernel(ib_ref, fb_ref, q_ref, k_ref, v_ref, og_ref, z_ref, ig_ref, fg_ref, nw_ref, o_ref, s_ref, m_ref):
    hp, nc = MLSTM_HEADS_PER_STEP, CHUNKS_PER_BLOCK
    hg = pl.program_id(1)

    @pl.when(pl.program_id(2) == 0)
    def _():
        s_ref[...] = jnp.zeros_like(s_ref)
        m_ref[...] = jnp.zeros_like(m_ref)

    q = q_ref[0].astype(F32)
    k = k_ref[0].astype(F32) * (HEAD_QK ** -0.5)

    r, c = _chunk_iotas()
    incl = r >= c
    upper = (r <= c).astype(F32)
    ones_col = (lax.broadcasted_iota(jnp.int32, (CHUNK, 128), 1) == 0).astype(BF16)
    chunk_id = lax.broadcasted_iota(jnp.int32, (nc, 1), 0)

    rows, d_rows, w_state, m_out = [], [], [], []
    for hd in range(hp):
        h = hg * hp + hd
        i_pre = ig_ref[0, hd] + ib_ref[h]
        lf = -_softplus(-(fg_ref[0, hd] + fb_ref[h]))
        bc = _dot_f32(lf, upper)
        b_last = bc[:, CHUNK - 1:CHUNK]
        d = i_pre - bc
        g_end = b_last + d
        g_max = jnp.max(g_end, axis=-1, keepdims=True)
        r_max = bc + _prefix_max_lanes(d)
        m = m_ref[hd, 0:1, 0:1]
        m_before = jnp.zeros((nc, 1), F32)
        m_after = jnp.zeros((nc, 1), F32)
        for n in range(nc):
            m_before = jnp.where(chunk_id == n, m, m_before)
            m = jnp.maximum(b_last[n:n + 1, :] + m, g_max[n:n + 1, :])
            m_after = jnp.where(chunk_id == n, m, m_after)
        m_out.append(m)
        m_t = jnp.maximum(bc + m_before, r_max)
        d_rows.append(d)
        w_state.append(jnp.exp(b_last + m_before - m_after))
        rows += [bc - m_t, jnp.exp(bc + m_before - m_t), jnp.exp(-m_t), jnp.exp(g_end - m_after)]
    cols = _columns(rows)

    def col(hd, kind, n, width):
        return _col(cols, (hd * _MLSTM_COLUMN_KINDS + kind) * nc + n, width)

    chains = [(hd, n) for n in range(nc) for hd in range(hp)]
    q_c = {ch: q[ch[1] * CHUNK:(ch[1] + 1) * CHUNK, ch[0] * HEAD_QK:(ch[0] + 1) * HEAD_QK] for ch in chains}
    k_c = {ch: k[ch[1] * CHUNK:(ch[1] + 1) * CHUNK, ch[0] * HEAD_QK:(ch[0] + 1) * HEAD_QK] for ch in chains}
    v_ext = {ch: jnp.concatenate([v_ref[0, ch[1] * CHUNK:(ch[1] + 1) * CHUNK, ch[0] * HEAD_V:(ch[0] + 1) * HEAD_V],
                                  ones_col], axis=1) for ch in chains}

    qk = {ch: _dot_nt(q_c[ch], k_c[ch]) for ch in chains}
    amat, kw = {}, {}
    for ch in chains:
        hd, n = ch
        arg = col(hd, 0, n, CHUNK) + d_rows[hd][n:n + 1, :]
        amat[ch] = (qk[ch] * jnp.exp(jnp.where(incl, arg, -jnp.inf))).astype(BF16)
        kw[ch] = (k_c[ch] * col(hd, 3, n, HEAD_QK)).astype(BF16)
    av = {ch: _dot(amat[ch], v_ext[ch]) for ch in chains}
    kv = {ch: _dot_tn(kw[ch], v_ext[ch]) for ch in chains}

    s_in = {}
    for hd in range(hp):
        s = s_ref[hd]
        for n in range(nc):
            s_in[(hd, n)] = s
            s = w_state[hd][n:n + 1, :] * s + kv[(hd, n)]
        s_ref[hd] = s
        m_ref[hd] = jnp.broadcast_to(m_out[hd], m_ref.shape[1:])
    qs = {ch: _dot(q_c[ch], s_in[ch]) for ch in chains}

    nw = nw_ref[...]
    for ch in chains:
        hd, n = ch
        lo = n * CHUNK
        num_ext = col(hd, 1, n, HEAD_V + 128) * qs[ch] + av[ch]
        denom = jnp.maximum(jnp.abs(num_ext[:, HEAD_V:HEAD_V + 1]), col(hd, 2, n, 1))
        hh = num_ext[:, :HEAD_V] / denom
        gate = (og_ref[0, lo:lo + CHUNK, hd * HEAD_V:(hd + 1) * HEAD_V].astype(F32)
                * z_ref[0, lo:lo + CHUNK, hd * HEAD_V:(hd + 1) * HEAD_V].astype(F32))
        o_ref[0, lo:lo + CHUNK, hd * HEAD_V:(hd + 1) * HEAD_V] = (_rms(hh, nw) * gate).astype(o_ref.dtype)


def _mlstm_recurrence(p3, ig_r, fg_r, i_bias, f_bias, norm_w):
    b, t, _ = p3.shape
    tb, hp = TIME_BLOCK, MLSTM_HEADS_PER_STEP
    wq, wv = hp * HEAD_QK, hp * HEAD_V
    nk = QK_W // wq
    nv = 2 * QK_W // wv
    nh = D_INNER // wv
    smem = pl.BlockSpec(memory_space=pltpu.SMEM)
    return pl.pallas_call(
        _mlstm_kernel,
        grid=(b, N_HEADS // hp, t // tb),
        in_specs=[
            smem, smem,
            pl.BlockSpec((1, tb, wq), lambda bi, h, ti: (bi, ti, h)),
            pl.BlockSpec((1, tb, wq), lambda bi, h, ti: (bi, ti, nk + h)),
            pl.BlockSpec((1, tb, wv), lambda bi, h, ti: (bi, ti, nv + h)),
            pl.BlockSpec((1, tb, wv), lambda bi, h, ti: (bi, ti, nv + nh + h)),
            pl.BlockSpec((1, tb, wv), lambda bi, h, ti: (bi, ti, nv + 2 * nh + h)),
            pl.BlockSpec((1, hp, CHUNKS_PER_BLOCK, CHUNK), lambda bi, h, ti: (bi, h, ti, 0)),
            pl.BlockSpec((1, hp, CHUNKS_PER_BLOCK, CHUNK), lambda bi, h, ti: (bi, h, ti, 0)),
            pl.BlockSpec((1, HEAD_V), lambda bi, h, ti: (0, 0)),
        ],
        out_specs=pl.BlockSpec((1, tb, wv), lambda bi, h, ti: (bi, ti, h)),
        out_shape=jax.ShapeDtypeStruct((b, t, D_INNER), BF16),
        scratch_shapes=[
            pltpu.VMEM((hp, HEAD_QK, HEAD_V + 128), F32),
            pltpu.VMEM((hp, 8, 128), F32),
        ],
        compiler_params=pltpu.CompilerParams(
            dimension_semantics=("parallel", "parallel", "arbitrary"),
            vmem_limit_bytes=VMEM_LIMIT_BYTES),
        name="mlstm_recurrence",
    )(i_bias, f_bias, p3, p3, p3, p3, p3, ig_r, fg_r, norm_w.reshape(1, HEAD_V))


def _tile_modes(*widths_and_modes):
    ranges, lo = [], 0
    for width, mode in widths_and_modes:
        ranges.append((lo, lo + width // COL_TILE, mode))
        lo += width // COL_TILE
    return tuple(ranges)


_GDN_TILE_MODES = _tile_modes((2 * QK_W + D_INNER, "conv_silu"), (D_INNER, "silu"))
_MLSTM_TILE_MODES = _tile_modes((2 * QK_W, "conv_silu"), (D_INNER, "none"), (D_INNER, "sigmoid"),
                                (D_INNER, "silu"))


def _gate_weight(w_in):
    n_main = w_in.shape[1] - 2 * N_HEADS
    return jnp.pad(w_in[:, n_main:], ((0, 0), (0, GATE_PAD - 2 * N_HEADS))).astype(BF16)


def _gate_rows(pg, lo, b, t):
    g = pg[:, lo:lo + N_HEADS].reshape(b, t, N_HEADS)
    return jnp.transpose(g, (0, 2, 1)).reshape(b, N_HEADS, t // CHUNK, CHUNK)


def kernel(x, norm_w, final_norm_w, gdn_w_in, gdn_conv_w, gdn_a_log, gdn_dt_bias, gdn_norm_w, gdn_w_out,
           mlstm_w_in, mlstm_conv_w, mlstm_i_bias, mlstm_f_bias, mlstm_norm_w, mlstm_w_out):
    b, t, d = x.shape
    depth = norm_w.shape[0]
    x2d = x.reshape(b * t, d)

    def in_weight(i):
        return (gdn_w_in if i % 2 == 0 else mlstm_w_in), i // 2

    w_in, j = in_weight(0)
    h, pg = _prenorm(x2d, norm_w[0], _gate_weight(w_in[j]))
    for i in range(depth):
        w_in, j = in_weight(i)
        gates = _gate_rows(pg, 0, b, t), _gate_rows(pg, N_HEADS, b, t)
        if i % 2 == 0:
            p = _inproj(h, w_in, j, gdn_conv_w[j], _GDN_TILE_MODES, t).reshape(b, t, -1)
            g = _gdn_recurrence(p, *gates, gdn_a_log[j], gdn_dt_bias[j], gdn_norm_w[j])
            w_out = gdn_w_out
        else:
            p = _inproj(h, w_in, j, mlstm_conv_w[j], _MLSTM_TILE_MODES, t).reshape(b, t, -1)
            g = _mlstm_recurrence(p, *gates, mlstm_i_bias[j], mlstm_f_bias[j], mlstm_norm_w[j])
            w_out = mlstm_w_out
        g = g.reshape(b * t, D_INNER)
        if i == depth - 1:
            x2d = _outproj_final(g, w_out, j, x2d, final_norm_w)
        else:
            w_next, j_next = in_weight(i + 1)
            x2d, h, pg = _outproj(g, w_out, j, x2d, norm_w[i + 1], _gate_weight(w_next[j_next]))
    return x2d.reshape(b, t, d)
```

```python
import functools

import jax
import jax.numpy as jnp
from jax import lax
from jax.experimental import pallas as pl
from jax.experimental.pallas import tpu as pltpu

F32 = jnp.float32
BF16 = jnp.bfloat16
HIGHEST = lax.Precision.HIGHEST

D_MODEL = 1024
N_HEADS = 8
HEAD_QK = 128
HEAD_V = 256
D_INNER = N_HEADS * HEAD_V
QK_W = N_HEADS * HEAD_QK
CONV_K = 4
CHUNK = 64
NORM_EPS = 1e-6
GATE_PAD = 128

INPROJ_ROW_TILE = 1024
INPROJ_SUB_ROWS = 128
ROW_TILE = 512
OUTPROJ_SUB_ROWS = 256
COL_TILE = 1024
TIME_BLOCK = 512
CHUNKS_PER_BLOCK = TIME_BLOCK // CHUNK
CONV_HALO = 8
VMEM_LIMIT_BYTES = 48 * 1024 * 1024


def _dot(a, b):
    return jnp.dot(a.astype(BF16), b.astype(BF16), preferred_element_type=F32)


def _dot_nt(a, b):
    return lax.dot_general(a.astype(BF16), b.astype(BF16), (((1,), (1,)), ((), ())),
                           preferred_element_type=F32)


def _dot_tn(a, b):
    return lax.dot_general(a.astype(BF16), b.astype(BF16), (((0,), (0,)), ((), ())),
                           preferred_element_type=F32)


def _dot_f32(a, b):
    return jnp.dot(a, b, precision=HIGHEST, preferred_element_type=F32)


def _sigmoid(x):
    return jax.nn.sigmoid(x)


def _silu(x):
    return x * _sigmoid(x)


def _softplus(x):
    return jnp.maximum(x, 0.0) + jnp.log1p(jnp.exp(-jnp.abs(x)))


def _rms(x, w):
    return x * lax.rsqrt(jnp.mean(x * x, axis=-1, keepdims=True) + NORM_EPS) * w


def _gates_t(wg_ref, h):
    return lax.dot_general(wg_ref[...].astype(BF16), h, (((1,), (1,)), ((), ())), preferred_element_type=F32)


def _prenorm_kernel(x_ref, nw_ref, wg_ref, h_ref, pg_ref):
    h = _rms(x_ref[...], nw_ref[...]).astype(BF16)
    h_ref[...] = h
    pg_ref[...] = _gates_t(wg_ref, h)


def _prenorm(x2d, nw, w_gate):
    m, d = x2d.shape
    return pl.pallas_call(
        _prenorm_kernel,
        grid=(m // ROW_TILE,),
        in_specs=[
            pl.BlockSpec((ROW_TILE, d), lambda i: (i, 0)),
            pl.BlockSpec((1, d), lambda i: (0, 0)),
            pl.BlockSpec((GATE_PAD, d), lambda i: (0, 0)),
        ],
        out_specs=[
            pl.BlockSpec((ROW_TILE, d), lambda i: (i, 0)),
            pl.BlockSpec((GATE_PAD, ROW_TILE), lambda i: (0, i)),
        ],
        out_shape=[jax.ShapeDtypeStruct((m, d), BF16), jax.ShapeDtypeStruct((GATE_PAD, m), F32)],
        compiler_params=pltpu.CompilerParams(
            dimension_semantics=("parallel",), vmem_limit_bytes=VMEM_LIMIT_BYTES),
        name="prenorm",
    )(x2d, nw.reshape(1, d), w_gate)


def _conv_taps(tail, y, cw):
    rows = y.shape[0]
    ext = jnp.concatenate([tail, y], axis=0)
    acc = cw[CONV_K - 1:CONV_K, :] * y
    for j in range(CONV_K - 1):
        lo = CONV_HALO - (CONV_K - 1) + j
        acc = acc + cw[j:j + 1, :] * ext[lo:lo + rows, :]
    return acc


_ACTIVATIONS = {"none": lambda y: y, "silu": _silu, "sigmoid": _sigmoid}


def _inproj_kernel(h_ref, w_ref, cw_ref, p_ref, wb_ref, tail_ref, *, tile_modes, row_tiles_per_seq):
    j, i = pl.program_id(0), pl.program_id(1)

    @pl.when(i == 0)
    def _():
        for c0 in range(0, COL_TILE, 128):
            wb_ref[:, c0:c0 + 128] = w_ref[c0:c0 + 128, :].T.astype(BF16)

    def matmul(r0):
        return jnp.dot(h_ref[r0:r0 + INPROJ_SUB_ROWS, :], wb_ref[...], preferred_element_type=F32)

    for lo, hi, mode in tile_modes:
        @pl.when((j >= lo) & (j < hi))
        def _(mode=mode):
            if mode != "conv_silu":
                for r0 in range(0, INPROJ_ROW_TILE, INPROJ_SUB_ROWS):
                    p_ref[r0:r0 + INPROJ_SUB_ROWS, :] = _ACTIVATIONS[mode](matmul(r0)).astype(p_ref.dtype)
                return

            @pl.when(i % row_tiles_per_seq == 0)
            def _():
                tail_ref[...] = jnp.zeros_like(tail_ref)

            cw = cw_ref[...]
            tail = tail_ref[...]
            for r0 in range(0, INPROJ_ROW_TILE, INPROJ_SUB_ROWS):
                y = matmul(r0)
                p_ref[r0:r0 + INPROJ_SUB_ROWS, :] = _silu(_conv_taps(tail, y, cw)).astype(p_ref.dtype)
                tail = y[INPROJ_SUB_ROWS - CONV_HALO:, :]
            tail_ref[...] = tail


def _inproj(h2d, w_in_t, layer, conv_w, tile_modes, seq_len):
    m, d = h2d.shape
    n_main = tile_modes[-1][1] * COL_TILE
    n_conv = conv_w.shape[1] // COL_TILE
    body = functools.partial(_inproj_kernel, tile_modes=tile_modes, row_tiles_per_seq=seq_len // INPROJ_ROW_TILE)
    return pl.pallas_call(
        body,
        grid=(n_main // COL_TILE, m // INPROJ_ROW_TILE),
        in_specs=[
            pl.BlockSpec((INPROJ_ROW_TILE, d), lambda j, i: (i, 0)),
            pl.BlockSpec((None, COL_TILE, d), lambda j, i: (layer, j, 0)),
            pl.BlockSpec((CONV_K, COL_TILE), lambda j, i: (0, jnp.minimum(j, n_conv - 1))),
        ],
        out_specs=pl.BlockSpec((INPROJ_ROW_TILE, COL_TILE), lambda j, i: (i, j)),
        out_shape=jax.ShapeDtypeStruct((m, n_main), BF16),
        scratch_shapes=[pltpu.VMEM((d, COL_TILE), BF16),
                        pltpu.VMEM((CONV_HALO, COL_TILE), F32)],
        compiler_params=pltpu.CompilerParams(
            dimension_semantics=("arbitrary", "arbitrary"), vmem_limit_bytes=VMEM_LIMIT_BYTES),
        name="inproj",
    )(h2d, w_in_t, conv_w)


def _outproj_kernel(g_ref, w_ref, x_ref, nw_ref, wg_ref, xo_ref, h_ref, pg_ref, wb_ref):
    @pl.when(pl.program_id(0) == 0)
    def _():
        wb_ref[...] = w_ref[...].astype(BF16)

    for r0 in range(0, ROW_TILE, OUTPROJ_SUB_ROWS):
        rows = slice(r0, r0 + OUTPROJ_SUB_ROWS)
        y = x_ref[rows, :] + jnp.dot(g_ref[rows, :], wb_ref[...], preferred_element_type=F32)
        xo_ref[rows, :] = y
        h = _rms(y, nw_ref[...]).astype(BF16)
        h_ref[rows, :] = h
        pg_ref[:, rows] = _gates_t(wg_ref, h)


def _outproj_final_kernel(g_ref, w_ref, x_ref, fw_ref, o_ref, wb_ref):
    @pl.when(pl.program_id(0) == 0)
    def _():
        wb_ref[...] = w_ref[...].astype(BF16)

    for r0 in range(0, ROW_TILE, OUTPROJ_SUB_ROWS):
        rows = slice(r0, r0 + OUTPROJ_SUB_ROWS)
        y = x_ref[rows, :] + jnp.dot(g_ref[rows, :], wb_ref[...], preferred_element_type=F32)
        o_ref[rows, :] = _rms(y, fw_ref[...])


def _outproj(g2d, w_out, layer, x2d, nw_next, w_gate_next):
    m, d = x2d.shape
    k = g2d.shape[1]
    row = lambda i: (i, 0)
    fixed = lambda i: (0, 0)
    return pl.pallas_call(
        _outproj_kernel,
        grid=(m // ROW_TILE,),
        in_specs=[
            pl.BlockSpec((ROW_TILE, k), row),
            pl.BlockSpec((None, k, d), lambda i: (layer, 0, 0)),
            pl.BlockSpec((ROW_TILE, d), row),
            pl.BlockSpec((1, d), fixed),
            pl.BlockSpec((GATE_PAD, d), fixed),
        ],
        out_specs=[
            pl.BlockSpec((ROW_TILE, d), row),
            pl.BlockSpec((ROW_TILE, d), row),
            pl.BlockSpec((GATE_PAD, ROW_TILE), lambda i: (0, i)),
        ],
        out_shape=[jax.ShapeDtypeStruct((m, d), F32), jax.ShapeDtypeStruct((m, d), BF16),
                   jax.ShapeDtypeStruct((GATE_PAD, m), F32)],
        scratch_shapes=[pltpu.VMEM((k, d), BF16)],
        compiler_params=pltpu.CompilerParams(
            dimension_semantics=("arbitrary",), vmem_limit_bytes=VMEM_LIMIT_BYTES),
        name="outproj",
    )(g2d, w_out, x2d, nw_next.reshape(1, d), w_gate_next)


def _outproj_final(g2d, w_out, layer, x2d, final_w):
    m, d = x2d.shape
    k = g2d.shape[1]
    row = lambda i: (i, 0)
    return pl.pallas_call(
        _outproj_final_kernel,
        grid=(m // ROW_TILE,),
        in_specs=[
            pl.BlockSpec((ROW_TILE, k), row),
            pl.BlockSpec((None, k, d), lambda i: (layer, 0, 0)),
            pl.BlockSpec((ROW_TILE, d), row),
            pl.BlockSpec((1, d), lambda i: (0, 0)),
        ],
        out_specs=pl.BlockSpec((ROW_TILE, d), row),
        out_shape=jax.ShapeDtypeStruct((m, d), F32),
        scratch_shapes=[pltpu.VMEM((k, d), BF16)],
        compiler_params=pltpu.CompilerParams(
            dimension_semantics=("arbitrary",), vmem_limit_bytes=VMEM_LIMIT_BYTES),
        name="outproj_final",
    )(g2d, w_out, x2d, final_w.reshape(1, d))


def _chunk_iotas():
    r = lax.broadcasted_iota(jnp.int32, (CHUNK, CHUNK), 0)
    c = lax.broadcasted_iota(jnp.int32, (CHUNK, CHUNK), 1)
    return r, c


def _gate_spec(heads, last_block, kind, ahead):
    return pl.BlockSpec((None, heads, None, CHUNKS_PER_BLOCK, CHUNK),
                        lambda bi, h, ti: (kind, h, bi, jnp.minimum(ti + ahead, last_block), 0))


def _col(xt, n, width):
    return jnp.broadcast_to(xt[:, n:n + 1], (CHUNK, width))


def _columns(rows_list):
    rows = jnp.concatenate(rows_list, axis=0)
    if rows.shape[0] < 128:
        rows = jnp.concatenate([rows, jnp.zeros((128 - rows.shape[0], CHUNK), F32)], axis=0)
    rows = jnp.concatenate([rows, jnp.zeros((128, 128 - CHUNK), F32)], axis=1)
    return rows.T[:CHUNK, :]


def _unit_lower_inverse_minus_identity(a_list, r, c):
    base = 8
    blk = (r // base) == (c // base)
    a0 = [jnp.where(blk, a, 0.0) for a in a_list]
    a2 = [_dot(x, x) for x in a0]
    a34 = [_dot(jnp.concatenate([x, y], axis=0), y) for x, y in zip(a0, a2)]
    p = [y - x - z[:CHUNK] for x, y, z in zip(a0, a2, a34)]
    a4 = [z[CHUNK:] for z in a34]
    n = [x + y + _dot(x, y) for x, y in zip(p, a4)]
    s = base
    while s < CHUNK:
        off_mask = ((r // (2 * s)) == (c // (2 * s))) & (((r // s) % 2) == 1) & (((c // s) % 2) == 0)
        off = [jnp.where(off_mask, a, 0.0) for a in a_list]
        x = [o + _dot(o, m) for o, m in zip(off, n)]
        n = [m - y - _dot(m, y) for m, y in zip(n, x)]
        s *= 2
    return n


GDN_HEADS_PER_STEP = 4
_GDN_COLUMN_KINDS = 5


def _gdn_kernel(alog_ref, dtb_ref, q_ref, k_ref, v_ref, z_ref, a_ref, bt_ref, nw_ref, o_ref, s_ref):
    hp, nc = GDN_HEADS_PER_STEP, CHUNKS_PER_BLOCK
    hg = pl.program_id(1)

    @pl.when(pl.program_id(2) == 0)
    def _():
        s_ref[...] = jnp.zeros_like(s_ref)

    q = q_ref[0].astype(F32)
    k = k_ref[0].astype(F32)
    v = v_ref[0].astype(F32)

    r, c = _chunk_iotas()
    incl = r >= c
    strict = r > c
    upper = (r <= c).astype(F32)

    qh, kh, vh, gcs, es, rows = [], [], [], [], [], []
    for hd in range(hp):
        h = hg * hp + hd
        qs = q[:, hd * HEAD_QK:(hd + 1) * HEAD_QK]
        ks = k[:, hd * HEAD_QK:(hd + 1) * HEAD_QK]
        qh.append(qs * lax.rsqrt(jnp.sum(qs * qs, axis=-1, keepdims=True) + 1e-6) * (HEAD_QK ** -0.5))
        kh.append(ks * lax.rsqrt(jnp.sum(ks * ks, axis=-1, keepdims=True) + 1e-6))
        vh.append(v[:, hd * HEAD_V:(hd + 1) * HEAD_V])
        neg_rate = -jnp.exp(jnp.full((1, 1), alog_ref[h], F32))
        g = neg_rate * _softplus(a_ref[hd] + dtb_ref[h])
        beta = _sigmoid(bt_ref[hd])
        gc = _dot_f32(g, upper)
        e = jnp.exp(gc)
        gcs.append(gc)
        es.append(e)
        rows += [gc, beta, beta * e, e, jnp.exp(gc[:, CHUNK - 1:CHUNK] - gc)]
    heads_per_transpose = 128 // (_GDN_COLUMN_KINDS * nc)
    cols = [_columns(rows[i * _GDN_COLUMN_KINDS:(i + heads_per_transpose) * _GDN_COLUMN_KINDS])
            for i in range(0, hp, heads_per_transpose)]

    def col(hd, kind, n, width):
        i = ((hd % heads_per_transpose) * _GDN_COLUMN_KINDS + kind) * nc + n
        return _col(cols[hd // heads_per_transpose], i, width)

    chains = [(hd, n) for n in range(nc) for hd in range(hp)]
    q_c = {ch: qh[ch[0]][ch[1] * CHUNK:(ch[1] + 1) * CHUNK] for ch in chains}
    k_c = {ch: kh[ch[0]][ch[1] * CHUNK:(ch[1] + 1) * CHUNK] for ch in chains}
    v_c = {ch: vh[ch[0]][ch[1] * CHUNK:(ch[1] + 1) * CHUNK] for ch in chains}

    kq = {ch: _dot_nt(jnp.concatenate([k_c[ch], q_c[ch]], axis=0), k_c[ch]) for ch in chains}
    a_list, attn = [], {}
    for ch in chains:
        hd, n = ch
        gc_row = gcs[hd][n:n + 1, :]
        decay = jnp.where(incl, jnp.exp(jnp.where(incl, col(hd, 0, n, CHUNK) - gc_row, 0.0)), 0.0)
        a_list.append(jnp.where(strict, col(hd, 1, n, CHUNK) * kq[ch][:CHUNK] * decay, 0.0))
        attn[ch] = (kq[ch][CHUNK:] * decay).astype(BF16)
    n_list = _unit_lower_inverse_minus_identity(a_list, r, c)
    uw, kd = {}, {}
    for ch, nmat in zip(chains, n_list):
        hd, n = ch
        rhs = jnp.concatenate([col(hd, 1, n, HEAD_V) * v_c[ch], col(hd, 2, n, HEAD_QK) * k_c[ch]], axis=1)
        uw[ch] = (rhs + _dot(nmat, rhs)).astype(BF16)
        kd[ch] = (k_c[ch] * col(hd, 4, n, HEAD_QK)).astype(BF16)
    kd_uw = {ch: _dot_tn(kd[ch], uw[ch]) for ch in chains}
    at_uw = {ch: _dot(attn[ch], uw[ch]) for ch in chains}
    lhs = {}
    for ch in chains:
        hd, n = ch
        to_out = q_c[ch] * col(hd, 3, n, HEAD_QK) - at_uw[ch][:, HEAD_V:]
        lhs[ch] = jnp.concatenate([-kd_uw[ch][:, HEAD_V:], to_out], axis=0).astype(BF16)

    nw = nw_ref[...]
    s = [s_ref[hd] for hd in range(hp)]
    for n in range(nc):
        lo = n * CHUNK
        m = [_dot(lhs[(hd, n)], s[hd]) for hd in range(hp)]
        o = [m[hd][HEAD_QK:] + at_uw[(hd, n)][:, :HEAD_V] for hd in range(hp)]
        s = [es[hd][n:n + 1, CHUNK - 1:CHUNK] * s[hd] + m[hd][:HEAD_QK] + kd_uw[(hd, n)][:, :HEAD_V]
             for hd in range(hp)]
        for hd in range(hp):
            z = z_ref[0, lo:lo + CHUNK, hd * HEAD_V:(hd + 1) * HEAD_V]
            o_ref[0, lo:lo + CHUNK, hd * HEAD_V:(hd + 1) * HEAD_V] = (_rms(o[hd], nw) * z.astype(F32)).astype(o_ref.dtype)
    for hd in range(hp):
        s_ref[hd] = s[hd]


def _gdn_recurrence(p3, gates, a_log, dt_bias, norm_w):
    b, t, _ = p3.shape
    tb, hp = TIME_BLOCK, GDN_HEADS_PER_STEP
    wq, wv = hp * HEAD_QK, hp * HEAD_V
    nk = QK_W // wq
    nv = 2 * QK_W // wv
    nz = (2 * QK_W + D_INNER) // wv
    smem = pl.BlockSpec(memory_space=pltpu.SMEM)
    gate_spec = functools.partial(_gate_spec, hp, t // tb - 1)
    return pl.pallas_call(
        _gdn_kernel,
        grid=(b, N_HEADS // hp, t // tb),
        in_specs=[
            smem, smem,
            pl.BlockSpec((1, tb, wq), lambda bi, h, ti: (bi, ti, h)),
            pl.BlockSpec((1, tb, wq), lambda bi, h, ti: (bi, ti, nk + h)),
            pl.BlockSpec((1, tb, wv), lambda bi, h, ti: (bi, ti, nv + h)),
            pl.BlockSpec((1, tb, wv), lambda bi, h, ti: (bi, ti, nz + h)),
            gate_spec(0, 0), gate_spec(1, 0),
            pl.BlockSpec((1, HEAD_V), lambda bi, h, ti: (0, 0)),
        ],
        out_specs=pl.BlockSpec((1, tb, wv), lambda bi, h, ti: (bi, ti, h)),
        out_shape=jax.ShapeDtypeStruct((b, t, D_INNER), BF16),
        scratch_shapes=[pltpu.VMEM((hp, HEAD_QK, HEAD_V), F32)],
        compiler_params=pltpu.CompilerParams(
            dimension_semantics=("parallel", "parallel", "arbitrary"),
            vmem_limit_bytes=VMEM_LIMIT_BYTES),
        name="gdn_recurrence",
    )(a_log, dt_bias, p3, p3, p3, p3, gates, gates, norm_w.reshape(1, HEAD_V))


MLSTM_HEADS_PER_STEP = 2
_MLSTM_COLUMN_KINDS = 4


def _prefix_max_lanes(x):
    y = jnp.concatenate([x, jnp.full((x.shape[0], 128 - CHUNK), -jnp.inf, F32)], axis=1)
    s = 1
    while s < CHUNK:
        y = jnp.maximum(y, pltpu.roll(y, s, axis=1))
        s *= 2
    return y[:, :CHUNK]


MLSTM_CHUNK_GROUPS = 4


def _mlstm_gate_setup(ig_ref, fg_ref, ib_ref, fb_ref, hg, m_ref, cols_ref, d_ref, ws_ref):
    hp, nc = MLSTM_HEADS_PER_STEP, CHUNKS_PER_BLOCK
    r, c = _chunk_iotas()
    upper = (r <= c).astype(F32)
    chunk_id = lax.broadcasted_iota(jnp.int32, (nc, 1), 0)
    rows = []
    for hd in range(hp):
        h = hg * hp + hd
        i_pre = ig_ref[hd] + ib_ref[h]
        lf = -_softplus(-(fg_ref[hd] + fb_ref[h]))
        bc = _dot_f32(lf, upper)
        b_last = bc[:, CHUNK - 1:CHUNK]
        d = i_pre - bc
        g_end = b_last + d
        g_max = jnp.max(g_end, axis=-1, keepdims=True)
        r_max = bc + _prefix_max_lanes(d)
        m = m_ref[hd, 0:1, 0:1]
        m_before = jnp.zeros((nc, 1), F32)
        m_after = jnp.zeros((nc, 1), F32)
        for n in range(nc):
            m_before = jnp.where(chunk_id == n, m, m_before)
            m = jnp.maximum(b_last[n:n + 1, :] + m, g_max[n:n + 1, :])
            m_after = jnp.where(chunk_id == n, m, m_after)
        m_ref[hd] = jnp.broadcast_to(m, m_ref.shape[1:])
        m_t = jnp.maximum(bc + m_before, r_max)
        d_ref[hd] = d
        ws_ref[hd] = jnp.broadcast_to(jnp.exp(b_last + m_before - m_after), ws_ref.shape[1:])
        rows += [bc - m_t, jnp.exp(bc + m_before - m_t), jnp.exp(-m_t), jnp.exp(g_end - m_after)]
    cols_ref[...] = _columns(rows)


def _mlstm_kernel(ib_ref, fb_ref, q_ref, k_ref, v_ref, og_ref, z_ref, ig_ref, fg_ref, ig_next_ref, fg_next_ref,
                  nw_ref, o_ref, s_ref, m_ref, cols_ref, d_ref, ws_ref):
    hp, nc = MLSTM_HEADS_PER_STEP, CHUNKS_PER_BLOCK
    hg = pl.program_id(1)
    setup = functools.partial(_mlstm_gate_setup, ib_ref=ib_ref, fb_ref=fb_ref, hg=hg, m_ref=m_ref,
                              cols_ref=cols_ref, d_ref=d_ref, ws_ref=ws_ref)

    @pl.when(pl.program_id(2) == 0)
    def _():
        s_ref[...] = jnp.zeros_like(s_ref)
        m_ref[...] = jnp.zeros_like(m_ref)
        setup(ig_ref, fg_ref)

    cols = cols_ref[...]
    d_rows = [d_ref[hd] for hd in range(hp)]
    w_state = [ws_ref[hd][:, 0:1] for hd in range(hp)]
    setup(ig_next_ref, fg_next_ref)

    q = q_ref[0].astype(F32)
    k = k_ref[0].astype(F32) * (HEAD_QK ** -0.5)

    r, c = _chunk_iotas()
    incl = r >= c
    ones_col = (lax.broadcasted_iota(jnp.int32, (CHUNK, 128), 1) == 0).astype(BF16)

    def col(hd, kind, n, width):
        return _col(cols, (hd * _MLSTM_COLUMN_KINDS + kind) * nc + n, width)

    nw = nw_ref[...]
    s = [s_ref[hd] for hd in range(hp)]
    for grp in range(MLSTM_CHUNK_GROUPS):
        chunks = range(grp * nc // MLSTM_CHUNK_GROUPS, (grp + 1) * nc // MLSTM_CHUNK_GROUPS)
        chains = [(hd, n) for n in chunks for hd in range(hp)]
        q_c = {ch: q[ch[1] * CHUNK:(ch[1] + 1) * CHUNK, ch[0] * HEAD_QK:(ch[0] + 1) * HEAD_QK] for ch in chains}
        k_c = {ch: k[ch[1] * CHUNK:(ch[1] + 1) * CHUNK, ch[0] * HEAD_QK:(ch[0] + 1) * HEAD_QK] for ch in chains}
        v_ext = {ch: jnp.concatenate([v_ref[0, ch[1] * CHUNK:(ch[1] + 1) * CHUNK, ch[0] * HEAD_V:(ch[0] + 1) * HEAD_V],
                                      ones_col], axis=1) for ch in chains}

        qk = {ch: _dot_nt(q_c[ch], k_c[ch]) for ch in chains}
        amat, kw = {}, {}
        for ch in chains:
            hd, n = ch
            arg = col(hd, 0, n, CHUNK) + d_rows[hd][n:n + 1, :]
            amat[ch] = (qk[ch] * jnp.exp(jnp.where(incl, arg, -jnp.inf))).astype(BF16)
            kw[ch] = (k_c[ch] * col(hd, 3, n, HEAD_QK)).astype(BF16)
        av = {ch: _dot(amat[ch], v_ext[ch]) for ch in chains}
        kv = {ch: _dot_tn(kw[ch], v_ext[ch]) for ch in chains}

        s_in = {}
        for hd in range(hp):
            for n in chunks:
                s_in[(hd, n)] = s[hd]
                s[hd] = w_state[hd][n:n + 1, :] * s[hd] + kv[(hd, n)]
        qs = {ch: _dot(q_c[ch], s_in[ch]) for ch in chains}

        for ch in chains:
            hd, n = ch
            lo = n * CHUNK
            num_ext = col(hd, 1, n, HEAD_V + 128) * qs[ch] + av[ch]
            denom = jnp.maximum(jnp.abs(num_ext[:, HEAD_V:HEAD_V + 1]), col(hd, 2, n, 1))
            hh = num_ext[:, :HEAD_V] / denom
            gate = (og_ref[0, lo:lo + CHUNK, hd * HEAD_V:(hd + 1) * HEAD_V].astype(F32)
                    * z_ref[0, lo:lo + CHUNK, hd * HEAD_V:(hd + 1) * HEAD_V].astype(F32))
            o_ref[0, lo:lo + CHUNK, hd * HEAD_V:(hd + 1) * HEAD_V] = (_rms(hh, nw) * gate).astype(o_ref.dtype)
    for hd in range(hp):
        s_ref[hd] = s[hd]


def _mlstm_recurrence(p3, gates, i_bias, f_bias, norm_w):
    b, t, _ = p3.shape
    tb, hp = TIME_BLOCK, MLSTM_HEADS_PER_STEP
    wq, wv = hp * HEAD_QK, hp * HEAD_V
    nk = QK_W // wq
    nv = 2 * QK_W // wv
    nh = D_INNER // wv
    smem = pl.BlockSpec(memory_space=pltpu.SMEM)
    gate_spec = functools.partial(_gate_spec, hp, t // tb - 1)
    return pl.pallas_call(
        _mlstm_kernel,
        grid=(b, N_HEADS // hp, t // tb),
        in_specs=[
            smem, smem,
            pl.BlockSpec((1, tb, wq), lambda bi, h, ti: (bi, ti, h)),
            pl.BlockSpec((1, tb, wq), lambda bi, h, ti: (bi, ti, nk + h)),
            pl.BlockSpec((1, tb, wv), lambda bi, h, ti: (bi, ti, nv + h)),
            pl.BlockSpec((1, tb, wv), lambda bi, h, ti: (bi, ti, nv + nh + h)),
            pl.BlockSpec((1, tb, wv), lambda bi, h, ti: (bi, ti, nv + 2 * nh + h)),
            gate_spec(0, 0), gate_spec(1, 0), gate_spec(0, 1), gate_spec(1, 1),
            pl.BlockSpec((1, HEAD_V), lambda bi, h, ti: (0, 0)),
        ],
        out_specs=pl.BlockSpec((1, tb, wv), lambda bi, h, ti: (bi, ti, h)),
        out_shape=jax.ShapeDtypeStruct((b, t, D_INNER), BF16),
        scratch_shapes=[
            pltpu.VMEM((hp, HEAD_QK, HEAD_V + 128), F32),
            pltpu.VMEM((hp, 8, 128), F32),
            pltpu.VMEM((CHUNK, 128), F32),
            pltpu.VMEM((hp, CHUNKS_PER_BLOCK, CHUNK), F32),
            pltpu.VMEM((hp, CHUNKS_PER_BLOCK, 128), F32),
        ],
        compiler_params=pltpu.CompilerParams(
            dimension_semantics=("parallel", "parallel", "arbitrary"),
            vmem_limit_bytes=VMEM_LIMIT_BYTES),
        name="mlstm_recurrence",
    )(i_bias, f_bias, p3, p3, p3, p3, p3, gates, gates, gates, gates, norm_w.reshape(1, HEAD_V))


def _tile_modes(*widths_and_modes):
    ranges, lo = [], 0
    for width, mode in widths_and_modes:
        ranges.append((lo, lo + width // COL_TILE, mode))
        lo += width // COL_TILE
    return tuple(ranges)


_GDN_TILE_MODES = _tile_modes((2 * QK_W + D_INNER, "conv_silu"), (D_INNER, "silu"))
_MLSTM_TILE_MODES = _tile_modes((2 * QK_W, "conv_silu"), (D_INNER, "none"), (D_INNER, "sigmoid"),
                                (D_INNER, "silu"))


def _gate_weight_t(w_in_t, layer):
    n_main = w_in_t.shape[1] - 2 * N_HEADS
    return jnp.pad(w_in_t[layer, n_main:, :], ((0, GATE_PAD - 2 * N_HEADS), (0, 0)))


def _gate_rows(pg_t, b, t):
    return pg_t[:2 * N_HEADS].reshape(2, N_HEADS, b, t // CHUNK, CHUNK)


def kernel(x, norm_w, final_norm_w, gdn_w_in, gdn_conv_w, gdn_a_log, gdn_dt_bias, gdn_norm_w, gdn_w_out,
           mlstm_w_in, mlstm_conv_w, mlstm_i_bias, mlstm_f_bias, mlstm_norm_w, mlstm_w_out):
    b, t, d = x.shape
    depth = norm_w.shape[0]
    x2d = x.reshape(b * t, d)
    w_in_t = (jnp.swapaxes(gdn_w_in, 1, 2), jnp.swapaxes(mlstm_w_in, 1, 2))

    h, pg = _prenorm(x2d, norm_w[0], _gate_weight_t(w_in_t[0], 0))
    for i in range(depth):
        j = i // 2
        gates = _gate_rows(pg, b, t)
        if i % 2 == 0:
            p = _inproj(h, w_in_t[0], j, gdn_conv_w[j], _GDN_TILE_MODES, t).reshape(b, t, -1)
            g = _gdn_recurrence(p, gates, gdn_a_log[j], gdn_dt_bias[j], gdn_norm_w[j])
            w_out = gdn_w_out
        else:
            p = _inproj(h, w_in_t[1], j, mlstm_conv_w[j], _MLSTM_TILE_MODES, t).reshape(b, t, -1)
            g = _mlstm_recurrence(p, gates, mlstm_i_bias[j], mlstm_f_bias[j], mlstm_norm_w[j])
            w_out = mlstm_w_out
        g = g.reshape(b * t, D_INNER)
        if i == depth - 1:
            x2d = _outproj_final(g, w_out, j, x2d, final_norm_w)
        else:
            x2d, h, pg = _outproj(g, w_out, j, x2d, norm_w[i + 1], _gate_weight_t(w_in_t[(i + 1) % 2], (i + 1) // 2))
    return x2d.reshape(b, t, d)
```

```python
import functools

import jax
import jax.numpy as jnp
from jax import lax
from jax.experimental import pallas as pl
from jax.experimental.pallas import tpu as pltpu

F32 = jnp.float32
BF16 = jnp.bfloat16
HIGHEST = lax.Precision.HIGHEST

D_MODEL = 1024
N_HEADS = 8
HEAD_QK = 128
HEAD_V = 256
D_INNER = N_HEADS * HEAD_V
QK_W = N_HEADS * HEAD_QK
CONV_K = 4
CHUNK = 64
NORM_EPS = 1e-6
GATE_PAD = 128

INPROJ_ROW_TILE = 1024
INPROJ_SUB_ROWS = 128
ROW_TILE = 512
OUTPROJ_SUB_ROWS = 256
COL_TILE = 1024
TIME_BLOCK = 512
CHUNKS_PER_BLOCK = TIME_BLOCK // CHUNK
CONV_HALO = 8
VMEM_LIMIT_BYTES = 48 * 1024 * 1024


def _dot(a, b):
    return jnp.dot(a.astype(BF16), b.astype(BF16), preferred_element_type=F32)


def _dot_nt(a, b):
    return lax.dot_general(a.astype(BF16), b.astype(BF16), (((1,), (1,)), ((), ())),
                           preferred_element_type=F32)


def _dot_tn(a, b):
    return lax.dot_general(a.astype(BF16), b.astype(BF16), (((0,), (0,)), ((), ())),
                           preferred_element_type=F32)


def _dot_f32(a, b):
    return jnp.dot(a, b, precision=HIGHEST, preferred_element_type=F32)


def _sigmoid(x):
    return jax.nn.sigmoid(x)


def _silu(x):
    return x * _sigmoid(x)


def _softplus(x):
    return jnp.maximum(x, 0.0) + jnp.log1p(jnp.exp(-jnp.abs(x)))


def _rms(x, w):
    return x * lax.rsqrt(jnp.mean(x * x, axis=-1, keepdims=True) + NORM_EPS) * w


def _gates_t(wg_ref, h):
    return lax.dot_general(wg_ref[...].astype(BF16), h, (((1,), (1,)), ((), ())), preferred_element_type=F32)


def _prenorm_kernel(x_ref, nw_ref, wg_ref, h_ref, pg_ref):
    h = _rms(x_ref[...], nw_ref[...]).astype(BF16)
    h_ref[...] = h
    pg_ref[...] = _gates_t(wg_ref, h)


def _prenorm(x2d, nw, w_gate):
    m, d = x2d.shape
    return pl.pallas_call(
        _prenorm_kernel,
        grid=(m // ROW_TILE,),
        in_specs=[
            pl.BlockSpec((ROW_TILE, d), lambda i: (i, 0)),
            pl.BlockSpec((1, d), lambda i: (0, 0)),
            pl.BlockSpec((GATE_PAD, d), lambda i: (0, 0)),
        ],
        out_specs=[
            pl.BlockSpec((ROW_TILE, d), lambda i: (i, 0)),
            pl.BlockSpec((GATE_PAD, ROW_TILE), lambda i: (0, i)),
        ],
        out_shape=[jax.ShapeDtypeStruct((m, d), BF16), jax.ShapeDtypeStruct((GATE_PAD, m), F32)],
        compiler_params=pltpu.CompilerParams(
            dimension_semantics=("parallel",), vmem_limit_bytes=VMEM_LIMIT_BYTES),
        name="prenorm",
    )(x2d, nw.reshape(1, d), w_gate)


def _conv_taps(tail, y, cw):
    rows = y.shape[0]
    ext = jnp.concatenate([tail, y], axis=0)
    acc = cw[CONV_K - 1:CONV_K, :] * y
    for j in range(CONV_K - 1):
        lo = CONV_HALO - (CONV_K - 1) + j
        acc = acc + cw[j:j + 1, :] * ext[lo:lo + rows, :]
    return acc


def _sigmoid_tanh(x):
    return 0.5 + 0.5 * jnp.tanh(0.5 * x)


def _silu_tanh(x):
    half = 0.5 * x
    return half + half * jnp.tanh(half)


_ACTIVATIONS = {False: {"none": lambda y: y, "silu": _silu, "sigmoid": _sigmoid},
                True: {"none": lambda y: y, "silu": _silu_tanh, "sigmoid": _sigmoid_tanh}}


def _inproj_kernel(h_ref, w_ref, cw_ref, p_ref, wb_ref, tail_ref, *, tile_modes, row_tiles_per_seq, sub_rows,
                   tanh_act):
    j, i = pl.program_id(0), pl.program_id(1)
    acts = _ACTIVATIONS[tanh_act]

    @pl.when(i == 0)
    def _():
        for c0 in range(0, COL_TILE, 128):
            wb_ref[:, c0:c0 + 128] = w_ref[c0:c0 + 128, :].T.astype(BF16)

    def matmul(r0):
        return jnp.dot(h_ref[r0:r0 + sub_rows, :], wb_ref[...], preferred_element_type=F32)

    for lo, hi, mode in tile_modes:
        @pl.when((j >= lo) & (j < hi))
        def _(mode=mode):
            if mode != "conv_silu":
                for r0 in range(0, INPROJ_ROW_TILE, sub_rows):
                    p_ref[r0:r0 + sub_rows, :] = acts[mode](matmul(r0)).astype(p_ref.dtype)
                return

            @pl.when(i % row_tiles_per_seq == 0)
            def _():
                tail_ref[...] = jnp.zeros_like(tail_ref)

            cw = cw_ref[...]
            tail = tail_ref[...]
            for r0 in range(0, INPROJ_ROW_TILE, sub_rows):
                y = matmul(r0)
                p_ref[r0:r0 + sub_rows, :] = acts["silu"](_conv_taps(tail, y, cw)).astype(p_ref.dtype)
                tail = y[sub_rows - CONV_HALO:, :]
            tail_ref[...] = tail


def _inproj(h2d, w_in_t, layer, conv_w, tile_modes, seq_len, sub_rows=INPROJ_SUB_ROWS, tanh_act=False):
    m, d = h2d.shape
    n_main = tile_modes[-1][1] * COL_TILE
    n_conv = conv_w.shape[1] // COL_TILE
    body = functools.partial(_inproj_kernel, tile_modes=tile_modes, row_tiles_per_seq=seq_len // INPROJ_ROW_TILE,
                             sub_rows=sub_rows, tanh_act=tanh_act)
    return pl.pallas_call(
        body,
        grid=(n_main // COL_TILE, m // INPROJ_ROW_TILE),
        in_specs=[
            pl.BlockSpec((INPROJ_ROW_TILE, d), lambda j, i: (i, 0)),
            pl.BlockSpec((None, COL_TILE, d), lambda j, i: (layer, j, 0)),
            pl.BlockSpec((CONV_K, COL_TILE), lambda j, i: (0, jnp.minimum(j, n_conv - 1))),
        ],
        out_specs=pl.BlockSpec((INPROJ_ROW_TILE, COL_TILE), lambda j, i: (i, j)),
        out_shape=jax.ShapeDtypeStruct((m, n_main), BF16),
        scratch_shapes=[pltpu.VMEM((d, COL_TILE), BF16),
                        pltpu.VMEM((CONV_HALO, COL_TILE), F32)],
        compiler_params=pltpu.CompilerParams(
            dimension_semantics=("arbitrary", "arbitrary"), vmem_limit_bytes=VMEM_LIMIT_BYTES),
        name="inproj",
    )(h2d, w_in_t, conv_w)


def _outproj_kernel(g_ref, w_ref, x_ref, nw_ref, wg_ref, xo_ref, h_ref, pg_ref, wb_ref, *, sub_rows):
    @pl.when(pl.program_id(0) == 0)
    def _():
        wb_ref[...] = w_ref[...].astype(BF16)

    for r0 in range(0, ROW_TILE, sub_rows):
        rows = slice(r0, r0 + sub_rows)
        y = x_ref[rows, :] + jnp.dot(g_ref[rows, :], wb_ref[...], preferred_element_type=F32)
        xo_ref[rows, :] = y
        h = _rms(y, nw_ref[...]).astype(BF16)
        h_ref[rows, :] = h
        pg_ref[:, rows] = _gates_t(wg_ref, h)


def _outproj_final_kernel(g_ref, w_ref, x_ref, fw_ref, o_ref, wb_ref):
    @pl.when(pl.program_id(0) == 0)
    def _():
        wb_ref[...] = w_ref[...].astype(BF16)

    for r0 in range(0, ROW_TILE, OUTPROJ_SUB_ROWS):
        rows = slice(r0, r0 + OUTPROJ_SUB_ROWS)
        y = x_ref[rows, :] + jnp.dot(g_ref[rows, :], wb_ref[...], preferred_element_type=F32)
        o_ref[rows, :] = _rms(y, fw_ref[...])


def _outproj(g2d, w_out, layer, x2d, nw_next, w_gate_next, sub_rows=OUTPROJ_SUB_ROWS):
    m, d = x2d.shape
    k = g2d.shape[1]
    row = lambda i: (i, 0)
    fixed = lambda i: (0, 0)
    return pl.pallas_call(
        functools.partial(_outproj_kernel, sub_rows=sub_rows),
        grid=(m // ROW_TILE,),
        in_specs=[
            pl.BlockSpec((ROW_TILE, k), row),
            pl.BlockSpec((None, k, d), lambda i: (layer, 0, 0)),
            pl.BlockSpec((ROW_TILE, d), row),
            pl.BlockSpec((1, d), fixed),
            pl.BlockSpec((GATE_PAD, d), fixed),
        ],
        out_specs=[
            pl.BlockSpec((ROW_TILE, d), row),
            pl.BlockSpec((ROW_TILE, d), row),
            pl.BlockSpec((GATE_PAD, ROW_TILE), lambda i: (0, i)),
        ],
        out_shape=[jax.ShapeDtypeStruct((m, d), F32), jax.ShapeDtypeStruct((m, d), BF16),
                   jax.ShapeDtypeStruct((GATE_PAD, m), F32)],
        scratch_shapes=[pltpu.VMEM((k, d), BF16)],
        compiler_params=pltpu.CompilerParams(
            dimension_semantics=("arbitrary",), vmem_limit_bytes=VMEM_LIMIT_BYTES),
        name="outproj",
    )(g2d, w_out, x2d, nw_next.reshape(1, d), w_gate_next)


def _outproj_final(g2d, w_out, layer, x2d, final_w):
    m, d = x2d.shape
    k = g2d.shape[1]
    row = lambda i: (i, 0)
    return pl.pallas_call(
        _outproj_final_kernel,
        grid=(m // ROW_TILE,),
        in_specs=[
            pl.BlockSpec((ROW_TILE, k), row),
            pl.BlockSpec((None, k, d), lambda i: (layer, 0, 0)),
            pl.BlockSpec((ROW_TILE, d), row),
            pl.BlockSpec((1, d), lambda i: (0, 0)),
        ],
        out_specs=pl.BlockSpec((ROW_TILE, d), row),
        out_shape=jax.ShapeDtypeStruct((m, d), F32),
        scratch_shapes=[pltpu.VMEM((k, d), BF16)],
        compiler_params=pltpu.CompilerParams(
            dimension_semantics=("arbitrary",), vmem_limit_bytes=VMEM_LIMIT_BYTES),
        name="outproj_final",
    )(g2d, w_out, x2d, final_w.reshape(1, d))


def _chunk_iotas():
    r = lax.broadcasted_iota(jnp.int32, (CHUNK, CHUNK), 0)
    c = lax.broadcasted_iota(jnp.int32, (CHUNK, CHUNK), 1)
    return r, c


def _gate_spec(heads, last_block, kind, ahead):
    return pl.BlockSpec((None, heads, None, CHUNKS_PER_BLOCK, CHUNK),
                        lambda bi, h, ti: (kind, h, bi, jnp.minimum(ti + ahead, last_block), 0))


def _col(xt, n, width):
    return jnp.broadcast_to(xt[:, n:n + 1], (CHUNK, width))


def _columns(rows_list):
    rows = jnp.concatenate(rows_list, axis=0)
    if rows.shape[0] < 128:
        rows = jnp.concatenate([rows, jnp.zeros((128 - rows.shape[0], CHUNK), F32)], axis=0)
    rows = jnp.concatenate([rows, jnp.zeros((128, 128 - CHUNK), F32)], axis=1)
    return rows.T[:CHUNK, :]


def _unit_lower_inverse_minus_identity(a_list, r, c):
    base = 8
    blk = (r // base) == (c // base)
    a0 = [jnp.where(blk, a, 0.0) for a in a_list]
    a2 = [_dot(x, x) for x in a0]
    a34 = [_dot(jnp.concatenate([x, y], axis=0), y) for x, y in zip(a0, a2)]
    p = [y - x - z[:CHUNK] for x, y, z in zip(a0, a2, a34)]
    a4 = [z[CHUNK:] for z in a34]
    n = [x + y + _dot(x, y) for x, y in zip(p, a4)]
    s = base
    while s < CHUNK:
        off_mask = ((r // (2 * s)) == (c // (2 * s))) & (((r // s) % 2) == 1) & (((c // s) % 2) == 0)
        off = [jnp.where(off_mask, a, 0.0) for a in a_list]
        x = [o + _dot(o, m) for o, m in zip(off, n)]
        n = [m - y - _dot(m, y) for m, y in zip(n, x)]
        s *= 2
    return n


GDN_HEADS_PER_STEP = 4
_GDN_COLUMN_KINDS = 5


def _gdn_kernel(alog_ref, dtb_ref, q_ref, k_ref, v_ref, z_ref, a_ref, bt_ref, nw_ref, o_ref, s_ref):
    hp, nc = GDN_HEADS_PER_STEP, CHUNKS_PER_BLOCK
    hg = pl.program_id(1)

    @pl.when(pl.program_id(2) == 0)
    def _():
        s_ref[...] = jnp.zeros_like(s_ref)

    q = q_ref[0].astype(F32)
    k = k_ref[0].astype(F32)
    v = v_ref[0].astype(F32)

    r, c = _chunk_iotas()
    incl = r >= c
    strict = r > c
    upper = (r <= c).astype(F32)

    qh, kh, vh, gcs, es, rows = [], [], [], [], [], []
    for hd in range(hp):
        h = hg * hp + hd
        qs = q[:, hd * HEAD_QK:(hd + 1) * HEAD_QK]
        ks = k[:, hd * HEAD_QK:(hd + 1) * HEAD_QK]
        qh.append(qs * lax.rsqrt(jnp.sum(qs * qs, axis=-1, keepdims=True) + 1e-6) * (HEAD_QK ** -0.5))
        kh.append(ks * lax.rsqrt(jnp.sum(ks * ks, axis=-1, keepdims=True) + 1e-6))
        vh.append(v[:, hd * HEAD_V:(hd + 1) * HEAD_V])
        neg_rate = -jnp.exp(jnp.full((1, 1), alog_ref[h], F32))
        g = neg_rate * _softplus(a_ref[hd] + dtb_ref[h])
        beta = _sigmoid(bt_ref[hd])
        gc = _dot_f32(g, upper)
        e = jnp.exp(gc)
        gcs.append(gc)
        es.append(e)
        rows += [gc, beta, beta * e, e, jnp.exp(gc[:, CHUNK - 1:CHUNK] - gc)]
    heads_per_transpose = 128 // (_GDN_COLUMN_KINDS * nc)
    cols = [_columns(rows[i * _GDN_COLUMN_KINDS:(i + heads_per_transpose) * _GDN_COLUMN_KINDS])
            for i in range(0, hp, heads_per_transpose)]

    def col(hd, kind, n, width):
        i = ((hd % heads_per_transpose) * _GDN_COLUMN_KINDS + kind) * nc + n
        return _col(cols[hd // heads_per_transpose], i, width)

    chains = [(hd, n) for n in range(nc) for hd in range(hp)]
    q_c = {ch: qh[ch[0]][ch[1] * CHUNK:(ch[1] + 1) * CHUNK] for ch in chains}
    k_c = {ch: kh[ch[0]][ch[1] * CHUNK:(ch[1] + 1) * CHUNK] for ch in chains}
    v_c = {ch: vh[ch[0]][ch[1] * CHUNK:(ch[1] + 1) * CHUNK] for ch in chains}

    kq = {ch: _dot_nt(jnp.concatenate([k_c[ch], q_c[ch]], axis=0), k_c[ch]) for ch in chains}
    a_list, attn = [], {}
    for ch in chains:
        hd, n = ch
        gc_row = gcs[hd][n:n + 1, :]
        decay = jnp.where(incl, jnp.exp(jnp.where(incl, col(hd, 0, n, CHUNK) - gc_row, 0.0)), 0.0)
        a_list.append(jnp.where(strict, col(hd, 1, n, CHUNK) * kq[ch][:CHUNK] * decay, 0.0))
        attn[ch] = (kq[ch][CHUNK:] * decay).astype(BF16)
    n_list = _unit_lower_inverse_minus_identity(a_list, r, c)
    uw, kd = {}, {}
    for ch, nmat in zip(chains, n_list):
        hd, n = ch
        rhs = jnp.concatenate([col(hd, 1, n, HEAD_V) * v_c[ch], col(hd, 2, n, HEAD_QK) * k_c[ch]], axis=1)
        uw[ch] = (rhs + _dot(nmat, rhs)).astype(BF16)
        kd[ch] = (k_c[ch] * col(hd, 4, n, HEAD_QK)).astype(BF16)
    kd_uw = {ch: _dot_tn(kd[ch], uw[ch]) for ch in chains}
    at_uw = {ch: _dot(attn[ch], uw[ch]) for ch in chains}
    lhs = {}
    for ch in chains:
        hd, n = ch
        to_out = q_c[ch] * col(hd, 3, n, HEAD_QK) - at_uw[ch][:, HEAD_V:]
        lhs[ch] = jnp.concatenate([-kd_uw[ch][:, HEAD_V:], to_out], axis=0).astype(BF16)

    nw = nw_ref[...]
    s = [s_ref[hd] for hd in range(hp)]
    for n in range(nc):
        lo = n * CHUNK
        m = [_dot(lhs[(hd, n)], s[hd]) for hd in range(hp)]
        o = [m[hd][HEAD_QK:] + at_uw[(hd, n)][:, :HEAD_V] for hd in range(hp)]
        s = [es[hd][n:n + 1, CHUNK - 1:CHUNK] * s[hd] + m[hd][:HEAD_QK] + kd_uw[(hd, n)][:, :HEAD_V]
             for hd in range(hp)]
        for hd in range(hp):
            z = z_ref[0, lo:lo + CHUNK, hd * HEAD_V:(hd + 1) * HEAD_V]
            o_ref[0, lo:lo + CHUNK, hd * HEAD_V:(hd + 1) * HEAD_V] = (_rms(o[hd], nw) * z.astype(F32)).astype(o_ref.dtype)
    for hd in range(hp):
        s_ref[hd] = s[hd]


def _gdn_recurrence(p3, gates, a_log, dt_bias, norm_w):
    b, t, _ = p3.shape
    tb, hp = TIME_BLOCK, GDN_HEADS_PER_STEP
    wq, wv = hp * HEAD_QK, hp * HEAD_V
    nk = QK_W // wq
    nv = 2 * QK_W // wv
    nz = (2 * QK_W + D_INNER) // wv
    smem = pl.BlockSpec(memory_space=pltpu.SMEM)
    gate_spec = functools.partial(_gate_spec, hp, t // tb - 1)
    return pl.pallas_call(
        _gdn_kernel,
        grid=(b, N_HEADS // hp, t // tb),
        in_specs=[
            smem, smem,
            pl.BlockSpec((1, tb, wq), lambda bi, h, ti: (bi, ti, h)),
            pl.BlockSpec((1, tb, wq), lambda bi, h, ti: (bi, ti, nk + h)),
            pl.BlockSpec((1, tb, wv), lambda bi, h, ti: (bi, ti, nv + h)),
            pl.BlockSpec((1, tb, wv), lambda bi, h, ti: (bi, ti, nz + h)),
            gate_spec(0, 0), gate_spec(1, 0),
            pl.BlockSpec((1, HEAD_V), lambda bi, h, ti: (0, 0)),
        ],
        out_specs=pl.BlockSpec((1, tb, wv), lambda bi, h, ti: (bi, ti, h)),
        out_shape=jax.ShapeDtypeStruct((b, t, D_INNER), BF16),
        scratch_shapes=[pltpu.VMEM((hp, HEAD_QK, HEAD_V), F32)],
        compiler_params=pltpu.CompilerParams(
            dimension_semantics=("parallel", "parallel", "arbitrary"),
            vmem_limit_bytes=VMEM_LIMIT_BYTES),
        name="gdn_recurrence",
    )(a_log, dt_bias, p3, p3, p3, p3, gates, gates, norm_w.reshape(1, HEAD_V))


MLSTM_HEADS_PER_STEP = 2
_MLSTM_COLUMN_KINDS = 4


def _prefix_max_lanes(x):
    y = jnp.concatenate([x, jnp.full((x.shape[0], 128 - CHUNK), -jnp.inf, F32)], axis=1)
    s = 1
    while s < CHUNK:
        y = jnp.maximum(y, pltpu.roll(y, s, axis=1))
        s *= 2
    return y[:, :CHUNK]


MLSTM_CHUNK_GROUPS = 4


def _mlstm_gate_setup(ig_ref, fg_ref, ib_ref, fb_ref, hg, m_ref, cols_ref, d_ref, ws_ref):
    hp, nc = MLSTM_HEADS_PER_STEP, CHUNKS_PER_BLOCK
    r, c = _chunk_iotas()
    upper = (r <= c).astype(F32)
    chunk_id = lax.broadcasted_iota(jnp.int32, (nc, 1), 0)
    rows = []
    for hd in range(hp):
        h = hg * hp + hd
        i_pre = ig_ref[hd] + ib_ref[h]
        lf = -_softplus(-(fg_ref[hd] + fb_ref[h]))
        bc = _dot_f32(lf, upper)
        b_last = bc[:, CHUNK - 1:CHUNK]
        d = i_pre - bc
        g_end = b_last + d
        g_max = jnp.max(g_end, axis=-1, keepdims=True)
        r_max = bc + _prefix_max_lanes(d)
        m = m_ref[hd, 0:1, 0:1]
        m_before = jnp.zeros((nc, 1), F32)
        m_after = jnp.zeros((nc, 1), F32)
        for n in range(nc):
            m_before = jnp.where(chunk_id == n, m, m_before)
            m = jnp.maximum(b_last[n:n + 1, :] + m, g_max[n:n + 1, :])
            m_after = jnp.where(chunk_id == n, m, m_after)
        m_ref[hd] = jnp.broadcast_to(m, m_ref.shape[1:])
        m_t = jnp.maximum(bc + m_before, r_max)
        d_ref[hd] = d
        ws_ref[hd] = jnp.broadcast_to(jnp.exp(b_last + m_before - m_after), ws_ref.shape[1:])
        rows += [bc - m_t, jnp.exp(bc + m_before - m_t), jnp.exp(-m_t), jnp.exp(g_end - m_after)]
    cols_ref[...] = _columns(rows)


def _mlstm_kernel(ib_ref, fb_ref, q_ref, k_ref, v_ref, og_ref, z_ref, ig_ref, fg_ref, ig_next_ref, fg_next_ref,
                  nw_ref, o_ref, s_ref, m_ref, cols_ref, d_ref, ws_ref):
    hp, nc = MLSTM_HEADS_PER_STEP, CHUNKS_PER_BLOCK
    hg = pl.program_id(1)
    setup = functools.partial(_mlstm_gate_setup, ib_ref=ib_ref, fb_ref=fb_ref, hg=hg, m_ref=m_ref,
                              cols_ref=cols_ref, d_ref=d_ref, ws_ref=ws_ref)

    @pl.when(pl.program_id(2) == 0)
    def _():
        s_ref[...] = jnp.zeros_like(s_ref)
        m_ref[...] = jnp.zeros_like(m_ref)
        setup(ig_ref, fg_ref)

    cols = cols_ref[...]
    d_rows = [d_ref[hd] for hd in range(hp)]
    w_state = [ws_ref[hd][:, 0:1] for hd in range(hp)]
    setup(ig_next_ref, fg_next_ref)

    q = q_ref[0].astype(F32)
    k = k_ref[0].astype(F32) * (HEAD_QK ** -0.5)

    r, c = _chunk_iotas()
    incl = r >= c
    ones_col = (lax.broadcasted_iota(jnp.int32, (CHUNK, 128), 1) == 0).astype(BF16)

    def col(hd, kind, n, width):
        return _col(cols, (hd * _MLSTM_COLUMN_KINDS + kind) * nc + n, width)

    nw = nw_ref[...]
    s = [s_ref[hd] for hd in range(hp)]
    for grp in range(MLSTM_CHUNK_GROUPS):
        chunks = range(grp * nc // MLSTM_CHUNK_GROUPS, (grp + 1) * nc // MLSTM_CHUNK_GROUPS)
        chains = [(hd, n) for n in chunks for hd in range(hp)]
        q_c = {ch: q[ch[1] * CHUNK:(ch[1] + 1) * CHUNK, ch[0] * HEAD_QK:(ch[0] + 1) * HEAD_QK] for ch in chains}
        k_c = {ch: k[ch[1] * CHUNK:(ch[1] + 1) * CHUNK, ch[0] * HEAD_QK:(ch[0] + 1) * HEAD_QK] for ch in chains}
        v_ext = {ch: jnp.concatenate([v_ref[0, ch[1] * CHUNK:(ch[1] + 1) * CHUNK, ch[0] * HEAD_V:(ch[0] + 1) * HEAD_V],
                                      ones_col], axis=1) for ch in chains}

        qk = {ch: _dot_nt(q_c[ch], k_c[ch]) for ch in chains}
        amat, kw = {}, {}
        for ch in chains:
            hd, n = ch
            arg = col(hd, 0, n, CHUNK) + d_rows[hd][n:n + 1, :]
            amat[ch] = (qk[ch] * jnp.exp(jnp.where(incl, arg, -jnp.inf))).astype(BF16)
            kw[ch] = (k_c[ch] * col(hd, 3, n, HEAD_QK)).astype(BF16)
        av = {ch: _dot(amat[ch], v_ext[ch]) for ch in chains}
        kv = {ch: _dot_tn(kw[ch], v_ext[ch]) for ch in chains}

        s_in = {}
        for hd in range(hp):
            for n in chunks:
                s_in[(hd, n)] = s[hd]
                s[hd] = w_state[hd][n:n + 1, :] * s[hd] + kv[(hd, n)]
        qs = {ch: _dot(q_c[ch], s_in[ch]) for ch in chains}

        for ch in chains:
            hd, n = ch
            lo = n * CHUNK
            num_ext = col(hd, 1, n, HEAD_V + 128) * qs[ch] + av[ch]
            denom = jnp.maximum(jnp.abs(num_ext[:, HEAD_V:HEAD_V + 1]), col(hd, 2, n, 1))
            hh = num_ext[:, :HEAD_V] / denom
            gate = (og_ref[0, lo:lo + CHUNK, hd * HEAD_V:(hd + 1) * HEAD_V].astype(F32)
                    * z_ref[0, lo:lo + CHUNK, hd * HEAD_V:(hd + 1) * HEAD_V].astype(F32))
            o_ref[0, lo:lo + CHUNK, hd * HEAD_V:(hd + 1) * HEAD_V] = (_rms(hh, nw) * gate).astype(o_ref.dtype)
    for hd in range(hp):
        s_ref[hd] = s[hd]


def _mlstm_recurrence(p3, gates, i_bias, f_bias, norm_w):
    b, t, _ = p3.shape
    tb, hp = TIME_BLOCK, MLSTM_HEADS_PER_STEP
    wq, wv = hp * HEAD_QK, hp * HEAD_V
    nk = QK_W // wq
    nv = 2 * QK_W // wv
    nh = D_INNER // wv
    smem = pl.BlockSpec(memory_space=pltpu.SMEM)
    gate_spec = functools.partial(_gate_spec, hp, t // tb - 1)
    return pl.pallas_call(
        _mlstm_kernel,
        grid=(b, N_HEADS // hp, t // tb),
        in_specs=[
            smem, smem,
            pl.BlockSpec((1, tb, wq), lambda bi, h, ti: (bi, ti, h)),
            pl.BlockSpec((1, tb, wq), lambda bi, h, ti: (bi, ti, nk + h)),
            pl.BlockSpec((1, tb, wv), lambda bi, h, ti: (bi, ti, nv + h)),
            pl.BlockSpec((1, tb, wv), lambda bi, h, ti: (bi, ti, nv + nh + h)),
            pl.BlockSpec((1, tb, wv), lambda bi, h, ti: (bi, ti, nv + 2 * nh + h)),
            gate_spec(0, 0), gate_spec(1, 0), gate_spec(0, 1), gate_spec(1, 1),
            pl.BlockSpec((1, HEAD_V), lambda bi, h, ti: (0, 0)),
        ],
        out_specs=pl.BlockSpec((1, tb, wv), lambda bi, h, ti: (bi, ti, h)),
        out_shape=jax.ShapeDtypeStruct((b, t, D_INNER), BF16),
        scratch_shapes=[
            pltpu.VMEM((hp, HEAD_QK, HEAD_V + 128), F32),
            pltpu.VMEM((hp, 8, 128), F32),
            pltpu.VMEM((CHUNK, 128), F32),
            pltpu.VMEM((hp, CHUNKS_PER_BLOCK, CHUNK), F32),
            pltpu.VMEM((hp, CHUNKS_PER_BLOCK, 128), F32),
        ],
        compiler_params=pltpu.CompilerParams(
            dimension_semantics=("parallel", "parallel", "arbitrary"),
            vmem_limit_bytes=VMEM_LIMIT_BYTES),
        name="mlstm_recurrence",
    )(i_bias, f_bias, p3, p3, p3, p3, p3, gates, gates, gates, gates, norm_w.reshape(1, HEAD_V))


def _tile_modes(*widths_and_modes):
    ranges, lo = [], 0
    for width, mode in widths_and_modes:
        ranges.append((lo, lo + width // COL_TILE, mode))
        lo += width // COL_TILE
    return tuple(ranges)


_GDN_TILE_MODES = _tile_modes((2 * QK_W + D_INNER, "conv_silu"), (D_INNER, "silu"))
_MLSTM_TILE_MODES = _tile_modes((2 * QK_W, "conv_silu"), (D_INNER, "none"), (D_INNER, "sigmoid"),
                                (D_INNER, "silu"))


def _gate_weight_t(w_in_t, layer):
    n_main = w_in_t.shape[1] - 2 * N_HEADS
    return jnp.pad(w_in_t[layer, n_main:, :], ((0, GATE_PAD - 2 * N_HEADS), (0, 0)))


def _gate_rows(pg_t, b, t):
    return pg_t[:2 * N_HEADS].reshape(2, N_HEADS, b, t // CHUNK, CHUNK)


def kernel(x, norm_w, final_norm_w, gdn_w_in, gdn_conv_w, gdn_a_log, gdn_dt_bias, gdn_norm_w, gdn_w_out,
           mlstm_w_in, mlstm_conv_w, mlstm_i_bias, mlstm_f_bias, mlstm_norm_w, mlstm_w_out):
    b, t, d = x.shape
    depth = norm_w.shape[0]
    x2d = x.reshape(b * t, d)
    w_in_t = (jnp.swapaxes(gdn_w_in, 1, 2), jnp.swapaxes(mlstm_w_in, 1, 2))

    inproj_arms = [dict(sub_rows=128, tanh_act=False), dict(sub_rows=128, tanh_act=True),
                   dict(sub_rows=256, tanh_act=False), dict(sub_rows=256, tanh_act=True)]
    outproj_arms = [256, 512, 128]

    h, pg = _prenorm(x2d, norm_w[0], _gate_weight_t(w_in_t[0], 0))
    for i in range(depth):
        j = i // 2
        gates = _gate_rows(pg, b, t)
        if i % 2 == 0:
            p = _inproj(h, w_in_t[0], j, gdn_conv_w[j], _GDN_TILE_MODES, t, **inproj_arms[i]).reshape(b, t, -1)
            g = _gdn_recurrence(p, gates, gdn_a_log[j], gdn_dt_bias[j], gdn_norm_w[j])
            w_out = gdn_w_out
        else:
            p = _inproj(h, w_in_t[1], j, mlstm_conv_w[j], _MLSTM_TILE_MODES, t, **inproj_arms[i]).reshape(b, t, -1)
            g = _mlstm_recurrence(p, gates, mlstm_i_bias[j], mlstm_f_bias[j], mlstm_norm_w[j])
            w_out = mlstm_w_out
        g = g.reshape(b * t, D_INNER)
        if i == depth - 1:
            x2d = _outproj_final(g, w_out, j, x2d, final_norm_w)
        else:
            x2d, h, pg = _outproj(g, w_out, j, x2d, norm_w[i + 1], _gate_weight_t(w_in_t[(i + 1) % 2], (i + 1) // 2),
                                  sub_rows=outproj_arms[i])
    return x2d.reshape(b, t, d)
```

```python
import functools

import jax
import jax.numpy as jnp
from jax import lax
from jax.experimental import pallas as pl
from jax.experimental.pallas import tpu as pltpu

F32 = jnp.float32
BF16 = jnp.bfloat16
HIGHEST = lax.Precision.HIGHEST

D_MODEL = 1024
N_HEADS = 8
HEAD_QK = 128
HEAD_V = 256
D_INNER = N_HEADS * HEAD_V
QK_W = N_HEADS * HEAD_QK
CONV_K = 4
CHUNK = 64
NORM_EPS = 1e-6
GATE_PAD = 128

INPROJ_ROW_TILE = 1024
INPROJ_SUB_ROWS = 128
ROW_TILE = 512
OUTPROJ_SUB_ROWS = 256
COL_TILE = 1024
TIME_BLOCK = 512
CHUNKS_PER_BLOCK = TIME_BLOCK // CHUNK
CONV_HALO = 8
VMEM_LIMIT_BYTES = 48 * 1024 * 1024


def _dot(a, b):
    return jnp.dot(a.astype(BF16), b.astype(BF16), preferred_element_type=F32)


def _dot_nt(a, b):
    return lax.dot_general(a.astype(BF16), b.astype(BF16), (((1,), (1,)), ((), ())),
                           preferred_element_type=F32)


def _dot_tn(a, b):
    return lax.dot_general(a.astype(BF16), b.astype(BF16), (((0,), (0,)), ((), ())),
                           preferred_element_type=F32)


def _dot_f32(a, b):
    return jnp.dot(a, b, precision=HIGHEST, preferred_element_type=F32)


def _sigmoid(x):
    return jax.nn.sigmoid(x)


def _silu(x):
    return x * _sigmoid(x)


def _softplus(x):
    return jnp.maximum(x, 0.0) + jnp.log1p(jnp.exp(-jnp.abs(x)))


def _rms(x, w):
    return x * lax.rsqrt(jnp.mean(x * x, axis=-1, keepdims=True) + NORM_EPS) * w


def _gates_t(wg_ref, h):
    return lax.dot_general(wg_ref[...].astype(BF16), h, (((1,), (1,)), ((), ())), preferred_element_type=F32)


def _prenorm_kernel(x_ref, nw_ref, wg_ref, h_ref, pg_ref):
    h = _rms(x_ref[...], nw_ref[...]).astype(BF16)
    h_ref[...] = h
    pg_ref[...] = _gates_t(wg_ref, h)


def _prenorm(x2d, nw, w_gate):
    m, d = x2d.shape
    return pl.pallas_call(
        _prenorm_kernel,
        grid=(m // ROW_TILE,),
        in_specs=[
            pl.BlockSpec((ROW_TILE, d), lambda i: (i, 0)),
            pl.BlockSpec((1, d), lambda i: (0, 0)),
            pl.BlockSpec((GATE_PAD, d), lambda i: (0, 0)),
        ],
        out_specs=[
            pl.BlockSpec((ROW_TILE, d), lambda i: (i, 0)),
            pl.BlockSpec((GATE_PAD, ROW_TILE), lambda i: (0, i)),
        ],
        out_shape=[jax.ShapeDtypeStruct((m, d), BF16), jax.ShapeDtypeStruct((GATE_PAD, m), F32)],
        compiler_params=pltpu.CompilerParams(
            dimension_semantics=("parallel",), vmem_limit_bytes=VMEM_LIMIT_BYTES),
        name="prenorm",
    )(x2d, nw.reshape(1, d), w_gate)


def _conv_taps(tail, y, cw):
    rows = y.shape[0]
    ext = jnp.concatenate([tail, y], axis=0)
    acc = cw[CONV_K - 1:CONV_K, :] * y
    for j in range(CONV_K - 1):
        lo = CONV_HALO - (CONV_K - 1) + j
        acc = acc + cw[j:j + 1, :] * ext[lo:lo + rows, :]
    return acc


def _sigmoid_tanh(x):
    return 0.5 + 0.5 * jnp.tanh(0.5 * x)


def _silu_tanh(x):
    half = 0.5 * x
    return half + half * jnp.tanh(half)


_ACTIVATIONS = {False: {"none": lambda y: y, "silu": _silu, "sigmoid": _sigmoid},
                True: {"none": lambda y: y, "silu": _silu_tanh, "sigmoid": _sigmoid_tanh}}


def _inproj_kernel(h_ref, w_ref, cw_ref, p_ref, wb_ref, tail_ref, *, tile_modes, row_tiles_per_seq, sub_rows,
                   tanh_act):
    j, i = pl.program_id(0), pl.program_id(1)
    acts = _ACTIVATIONS[tanh_act]

    @pl.when(i == 0)
    def _():
        for c0 in range(0, COL_TILE, 128):
            wb_ref[:, c0:c0 + 128] = w_ref[c0:c0 + 128, :].T.astype(BF16)

    def matmul(r0):
        return jnp.dot(h_ref[r0:r0 + sub_rows, :], wb_ref[...], preferred_element_type=F32)

    for lo, hi, mode in tile_modes:
        @pl.when((j >= lo) & (j < hi))
        def _(mode=mode):
            if mode != "conv_silu":
                for r0 in range(0, INPROJ_ROW_TILE, sub_rows):
                    p_ref[r0:r0 + sub_rows, :] = acts[mode](matmul(r0)).astype(p_ref.dtype)
                return

            @pl.when(i % row_tiles_per_seq == 0)
            def _():
                tail_ref[...] = jnp.zeros_like(tail_ref)

            cw = cw_ref[...]
            tail = tail_ref[...]
            for r0 in range(0, INPROJ_ROW_TILE, sub_rows):
                y = matmul(r0)
                p_ref[r0:r0 + sub_rows, :] = acts["silu"](_conv_taps(tail, y, cw)).astype(p_ref.dtype)
                tail = y[sub_rows - CONV_HALO:, :]
            tail_ref[...] = tail


def _inproj(h2d, w_in_t, layer, conv_w, tile_modes, seq_len, sub_rows=INPROJ_SUB_ROWS, tanh_act=False):
    m, d = h2d.shape
    n_main = tile_modes[-1][1] * COL_TILE
    n_conv = conv_w.shape[1] // COL_TILE
    body = functools.partial(_inproj_kernel, tile_modes=tile_modes, row_tiles_per_seq=seq_len // INPROJ_ROW_TILE,
                             sub_rows=sub_rows, tanh_act=tanh_act)
    return pl.pallas_call(
        body,
        grid=(n_main // COL_TILE, m // INPROJ_ROW_TILE),
        in_specs=[
            pl.BlockSpec((INPROJ_ROW_TILE, d), lambda j, i: (i, 0)),
            pl.BlockSpec((None, COL_TILE, d), lambda j, i: (layer, j, 0)),
            pl.BlockSpec((CONV_K, COL_TILE), lambda j, i: (0, jnp.minimum(j, n_conv - 1))),
        ],
        out_specs=pl.BlockSpec((INPROJ_ROW_TILE, COL_TILE), lambda j, i: (i, j)),
        out_shape=jax.ShapeDtypeStruct((m, n_main), BF16),
        scratch_shapes=[pltpu.VMEM((d, COL_TILE), BF16),
                        pltpu.VMEM((CONV_HALO, COL_TILE), F32)],
        compiler_params=pltpu.CompilerParams(
            dimension_semantics=("arbitrary", "arbitrary"), vmem_limit_bytes=VMEM_LIMIT_BYTES),
        name="inproj",
    )(h2d, w_in_t, conv_w)


def _outproj_kernel(g_ref, w_ref, x_ref, nw_ref, wg_ref, xo_ref, h_ref, pg_ref, wb_ref, *, sub_rows):
    @pl.when(pl.program_id(0) == 0)
    def _():
        wb_ref[...] = w_ref[...].astype(BF16)

    for r0 in range(0, x_ref.shape[0], sub_rows):
        rows = slice(r0, r0 + sub_rows)
        y = x_ref[rows, :] + jnp.dot(g_ref[rows, :], wb_ref[...], preferred_element_type=F32)
        xo_ref[rows, :] = y
        h = _rms(y, nw_ref[...]).astype(BF16)
        h_ref[rows, :] = h
        pg_ref[:, rows] = _gates_t(wg_ref, h)


def _outproj_final_kernel(g_ref, w_ref, x_ref, fw_ref, o_ref, wb_ref):
    @pl.when(pl.program_id(0) == 0)
    def _():
        wb_ref[...] = w_ref[...].astype(BF16)

    for r0 in range(0, ROW_TILE, OUTPROJ_SUB_ROWS):
        rows = slice(r0, r0 + OUTPROJ_SUB_ROWS)
        y = x_ref[rows, :] + jnp.dot(g_ref[rows, :], wb_ref[...], preferred_element_type=F32)
        o_ref[rows, :] = _rms(y, fw_ref[...])


def _outproj(g2d, w_out, layer, x2d, nw_next, w_gate_next, sub_rows=OUTPROJ_SUB_ROWS, row_tile=ROW_TILE):
    m, d = x2d.shape
    k = g2d.shape[1]
    row = lambda i: (i, 0)
    fixed = lambda i: (0, 0)
    return pl.pallas_call(
        functools.partial(_outproj_kernel, sub_rows=sub_rows),
        grid=(m // row_tile,),
        in_specs=[
            pl.BlockSpec((row_tile, k), row),
            pl.BlockSpec((None, k, d), lambda i: (layer, 0, 0)),
            pl.BlockSpec((row_tile, d), row),
            pl.BlockSpec((1, d), fixed),
            pl.BlockSpec((GATE_PAD, d), fixed),
        ],
        out_specs=[
            pl.BlockSpec((row_tile, d), row),
            pl.BlockSpec((row_tile, d), row),
            pl.BlockSpec((GATE_PAD, row_tile), lambda i: (0, i)),
        ],
        out_shape=[jax.ShapeDtypeStruct((m, d), F32), jax.ShapeDtypeStruct((m, d), BF16),
                   jax.ShapeDtypeStruct((GATE_PAD, m), F32)],
        scratch_shapes=[pltpu.VMEM((k, d), BF16)],
        compiler_params=pltpu.CompilerParams(
            dimension_semantics=("arbitrary",), vmem_limit_bytes=VMEM_LIMIT_BYTES),
        name="outproj",
    )(g2d, w_out, x2d, nw_next.reshape(1, d), w_gate_next)


def _outproj_final(g2d, w_out, layer, x2d, final_w):
    m, d = x2d.shape
    k = g2d.shape[1]
    row = lambda i: (i, 0)
    return pl.pallas_call(
        _outproj_final_kernel,
        grid=(m // ROW_TILE,),
        in_specs=[
            pl.BlockSpec((ROW_TILE, k), row),
            pl.BlockSpec((None, k, d), lambda i: (layer, 0, 0)),
            pl.BlockSpec((ROW_TILE, d), row),
            pl.BlockSpec((1, d), lambda i: (0, 0)),
        ],
        out_specs=pl.BlockSpec((ROW_TILE, d), row),
        out_shape=jax.ShapeDtypeStruct((m, d), F32),
        scratch_shapes=[pltpu.VMEM((k, d), BF16)],
        compiler_params=pltpu.CompilerParams(
            dimension_semantics=("arbitrary",), vmem_limit_bytes=VMEM_LIMIT_BYTES),
        name="outproj_final",
    )(g2d, w_out, x2d, final_w.reshape(1, d))


def _chunk_iotas():
    r = lax.broadcasted_iota(jnp.int32, (CHUNK, CHUNK), 0)
    c = lax.broadcasted_iota(jnp.int32, (CHUNK, CHUNK), 1)
    return r, c


def _gate_spec(heads, last_block, kind, ahead):
    return pl.BlockSpec((None, heads, None, CHUNKS_PER_BLOCK, CHUNK),
                        lambda bi, h, ti: (kind, h, bi, jnp.minimum(ti + ahead, last_block), 0))


def _col(xt, n, width):
    return jnp.broadcast_to(xt[:, n:n + 1], (CHUNK, width))


def _columns(rows_list):
    rows = jnp.concatenate(rows_list, axis=0)
    if rows.shape[0] < 128:
        rows = jnp.concatenate([rows, jnp.zeros((128 - rows.shape[0], CHUNK), F32)], axis=0)
    rows = jnp.concatenate([rows, jnp.zeros((128, 128 - CHUNK), F32)], axis=1)
    return rows.T[:CHUNK, :]


def _unit_lower_inverse_minus_identity(a_list, r, c):
    base = 8
    blk = (r // base) == (c // base)
    a0 = [jnp.where(blk, a, 0.0) for a in a_list]
    a2 = [_dot(x, x) for x in a0]
    a34 = [_dot(jnp.concatenate([x, y], axis=0), y) for x, y in zip(a0, a2)]
    p = [y - x - z[:CHUNK] for x, y, z in zip(a0, a2, a34)]
    a4 = [z[CHUNK:] for z in a34]
    n = [x + y + _dot(x, y) for x, y in zip(p, a4)]
    s = base
    while s < CHUNK:
        off_mask = ((r // (2 * s)) == (c // (2 * s))) & (((r // s) % 2) == 1) & (((c // s) % 2) == 0)
        off = [jnp.where(off_mask, a, 0.0) for a in a_list]
        x = [o + _dot(o, m) for o, m in zip(off, n)]
        n = [m - y - _dot(m, y) for m, y in zip(n, x)]
        s *= 2
    return n


GDN_HEADS_PER_STEP = 4
_GDN_COLUMN_KINDS = 5


def _gdn_kernel(alog_ref, dtb_ref, q_ref, k_ref, v_ref, z_ref, a_ref, bt_ref, nw_ref, o_ref, s_ref, *, hp):
    nc = CHUNKS_PER_BLOCK
    hg = pl.program_id(1)

    @pl.when(pl.program_id(2) == 0)
    def _():
        s_ref[...] = jnp.zeros_like(s_ref)

    q = q_ref[0].astype(F32)
    k = k_ref[0].astype(F32)
    v = v_ref[0].astype(F32)

    r, c = _chunk_iotas()
    incl = r >= c
    strict = r > c
    upper = (r <= c).astype(F32)

    qh, kh, vh, gcs, es, rows = [], [], [], [], [], []
    for hd in range(hp):
        h = hg * hp + hd
        qs = q[:, hd * HEAD_QK:(hd + 1) * HEAD_QK]
        ks = k[:, hd * HEAD_QK:(hd + 1) * HEAD_QK]
        qh.append(qs * lax.rsqrt(jnp.sum(qs * qs, axis=-1, keepdims=True) + 1e-6) * (HEAD_QK ** -0.5))
        kh.append(ks * lax.rsqrt(jnp.sum(ks * ks, axis=-1, keepdims=True) + 1e-6))
        vh.append(v[:, hd * HEAD_V:(hd + 1) * HEAD_V])
        neg_rate = -jnp.exp(jnp.full((1, 1), alog_ref[h], F32))
        g = neg_rate * _softplus(a_ref[hd] + dtb_ref[h])
        beta = _sigmoid(bt_ref[hd])
        gc = _dot_f32(g, upper)
        e = jnp.exp(gc)
        gcs.append(gc)
        es.append(e)
        rows += [gc, beta, beta * e, e, jnp.exp(gc[:, CHUNK - 1:CHUNK] - gc)]
    heads_per_transpose = 128 // (_GDN_COLUMN_KINDS * nc)
    cols = [_columns(rows[i * _GDN_COLUMN_KINDS:(i + heads_per_transpose) * _GDN_COLUMN_KINDS])
            for i in range(0, hp, heads_per_transpose)]

    def col(hd, kind, n, width):
        i = ((hd % heads_per_transpose) * _GDN_COLUMN_KINDS + kind) * nc + n
        return _col(cols[hd // heads_per_transpose], i, width)

    chains = [(hd, n) for n in range(nc) for hd in range(hp)]
    q_c = {ch: qh[ch[0]][ch[1] * CHUNK:(ch[1] + 1) * CHUNK] for ch in chains}
    k_c = {ch: kh[ch[0]][ch[1] * CHUNK:(ch[1] + 1) * CHUNK] for ch in chains}
    v_c = {ch: vh[ch[0]][ch[1] * CHUNK:(ch[1] + 1) * CHUNK] for ch in chains}

    kq = {ch: _dot_nt(jnp.concatenate([k_c[ch], q_c[ch]], axis=0), k_c[ch]) for ch in chains}
    a_list, attn = [], {}
    for ch in chains:
        hd, n = ch
        gc_row = gcs[hd][n:n + 1, :]
        decay = jnp.where(incl, jnp.exp(jnp.where(incl, col(hd, 0, n, CHUNK) - gc_row, 0.0)), 0.0)
        a_list.append(jnp.where(strict, col(hd, 1, n, CHUNK) * kq[ch][:CHUNK] * decay, 0.0))
        attn[ch] = (kq[ch][CHUNK:] * decay).astype(BF16)
    n_list = _unit_lower_inverse_minus_identity(a_list, r, c)
    uw, kd = {}, {}
    for ch, nmat in zip(chains, n_list):
        hd, n = ch
        rhs = jnp.concatenate([col(hd, 1, n, HEAD_V) * v_c[ch], col(hd, 2, n, HEAD_QK) * k_c[ch]], axis=1)
        uw[ch] = (rhs + _dot(nmat, rhs)).astype(BF16)
        kd[ch] = (k_c[ch] * col(hd, 4, n, HEAD_QK)).astype(BF16)
    kd_uw = {ch: _dot_tn(kd[ch], uw[ch]) for ch in chains}
    at_uw = {ch: _dot(attn[ch], uw[ch]) for ch in chains}
    lhs = {}
    for ch in chains:
        hd, n = ch
        to_out = q_c[ch] * col(hd, 3, n, HEAD_QK) - at_uw[ch][:, HEAD_V:]
        lhs[ch] = jnp.concatenate([-kd_uw[ch][:, HEAD_V:], to_out], axis=0).astype(BF16)

    nw = nw_ref[...]
    s = [s_ref[hd] for hd in range(hp)]
    for n in range(nc):
        lo = n * CHUNK
        m = [_dot(lhs[(hd, n)], s[hd]) for hd in range(hp)]
        o = [m[hd][HEAD_QK:] + at_uw[(hd, n)][:, :HEAD_V] for hd in range(hp)]
        s = [es[hd][n:n + 1, CHUNK - 1:CHUNK] * s[hd] + m[hd][:HEAD_QK] + kd_uw[(hd, n)][:, :HEAD_V]
             for hd in range(hp)]
        for hd in range(hp):
            z = z_ref[0, lo:lo + CHUNK, hd * HEAD_V:(hd + 1) * HEAD_V]
            o_ref[0, lo:lo + CHUNK, hd * HEAD_V:(hd + 1) * HEAD_V] = (_rms(o[hd], nw) * z.astype(F32)).astype(o_ref.dtype)
    for hd in range(hp):
        s_ref[hd] = s[hd]


def _gdn_recurrence(p3, gates, a_log, dt_bias, norm_w, hp=GDN_HEADS_PER_STEP):
    b, t, _ = p3.shape
    tb = TIME_BLOCK
    wq, wv = hp * HEAD_QK, hp * HEAD_V
    nk = QK_W // wq
    nv = 2 * QK_W // wv
    nz = (2 * QK_W + D_INNER) // wv
    smem = pl.BlockSpec(memory_space=pltpu.SMEM)
    gate_spec = functools.partial(_gate_spec, hp, t // tb - 1)
    return pl.pallas_call(
        functools.partial(_gdn_kernel, hp=hp),
        grid=(b, N_HEADS // hp, t // tb),
        in_specs=[
            smem, smem,
            pl.BlockSpec((1, tb, wq), lambda bi, h, ti: (bi, ti, h)),
            pl.BlockSpec((1, tb, wq), lambda bi, h, ti: (bi, ti, nk + h)),
            pl.BlockSpec((1, tb, wv), lambda bi, h, ti: (bi, ti, nv + h)),
            pl.BlockSpec((1, tb, wv), lambda bi, h, ti: (bi, ti, nz + h)),
            gate_spec(0, 0), gate_spec(1, 0),
            pl.BlockSpec((1, HEAD_V), lambda bi, h, ti: (0, 0)),
        ],
        out_specs=pl.BlockSpec((1, tb, wv), lambda bi, h, ti: (bi, ti, h)),
        out_shape=jax.ShapeDtypeStruct((b, t, D_INNER), BF16),
        scratch_shapes=[pltpu.VMEM((hp, HEAD_QK, HEAD_V), F32)],
        compiler_params=pltpu.CompilerParams(
            dimension_semantics=("parallel", "parallel", "arbitrary"),
            vmem_limit_bytes=VMEM_LIMIT_BYTES),
        name="gdn_recurrence",
    )(a_log, dt_bias, p3, p3, p3, p3, gates, gates, norm_w.reshape(1, HEAD_V))


MLSTM_HEADS_PER_STEP = 2
_MLSTM_COLUMN_KINDS = 4


def _prefix_max_lanes(x):
    y = jnp.concatenate([x, jnp.full((x.shape[0], 128 - CHUNK), -jnp.inf, F32)], axis=1)
    s = 1
    while s < CHUNK:
        y = jnp.maximum(y, pltpu.roll(y, s, axis=1))
        s *= 2
    return y[:, :CHUNK]


MLSTM_CHUNK_GROUPS = 4


def _mlstm_gate_setup(ig_ref, fg_ref, ib_ref, fb_ref, hg, m_ref, cols_ref, d_ref, ws_ref):
    hp, nc = MLSTM_HEADS_PER_STEP, CHUNKS_PER_BLOCK
    r, c = _chunk_iotas()
    upper = (r <= c).astype(F32)
    chunk_id = lax.broadcasted_iota(jnp.int32, (nc, 1), 0)
    rows = []
    for hd in range(hp):
        h = hg * hp + hd
        i_pre = ig_ref[hd] + ib_ref[h]
        lf = -_softplus(-(fg_ref[hd] + fb_ref[h]))
        bc = _dot_f32(lf, upper)
        b_last = bc[:, CHUNK - 1:CHUNK]
        d = i_pre - bc
        g_end = b_last + d
        g_max = jnp.max(g_end, axis=-1, keepdims=True)
        r_max = bc + _prefix_max_lanes(d)
        m = m_ref[hd, 0:1, 0:1]
        m_before = jnp.zeros((nc, 1), F32)
        m_after = jnp.zeros((nc, 1), F32)
        for n in range(nc):
            m_before = jnp.where(chunk_id == n, m, m_before)
            m = jnp.maximum(b_last[n:n + 1, :] + m, g_max[n:n + 1, :])
            m_after = jnp.where(chunk_id == n, m, m_after)
        m_ref[hd] = jnp.broadcast_to(m, m_ref.shape[1:])
        m_t = jnp.maximum(bc + m_before, r_max)
        d_ref[hd] = d
        ws_ref[hd] = jnp.broadcast_to(jnp.exp(b_last + m_before - m_after), ws_ref.shape[1:])
        rows += [bc - m_t, jnp.exp(bc + m_before - m_t), jnp.exp(-m_t), jnp.exp(g_end - m_after)]
    cols_ref[...] = _columns(rows)


def _mlstm_kernel(ib_ref, fb_ref, q_ref, k_ref, v_ref, og_ref, z_ref, ig_ref, fg_ref, ig_next_ref, fg_next_ref,
                  nw_ref, o_ref, s_ref, m_ref, cols_ref, d_ref, ws_ref, *, groups):
    hp, nc = MLSTM_HEADS_PER_STEP, CHUNKS_PER_BLOCK
    hg = pl.program_id(1)
    setup = functools.partial(_mlstm_gate_setup, ib_ref=ib_ref, fb_ref=fb_ref, hg=hg, m_ref=m_ref,
                              cols_ref=cols_ref, d_ref=d_ref, ws_ref=ws_ref)

    @pl.when(pl.program_id(2) == 0)
    def _():
        s_ref[...] = jnp.zeros_like(s_ref)
        m_ref[...] = jnp.zeros_like(m_ref)
        setup(ig_ref, fg_ref)

    cols = cols_ref[...]
    d_rows = [d_ref[hd] for hd in range(hp)]
    w_state = [ws_ref[hd][:, 0:1] for hd in range(hp)]
    setup(ig_next_ref, fg_next_ref)

    q = q_ref[0].astype(F32)
    k = k_ref[0].astype(F32) * (HEAD_QK ** -0.5)

    r, c = _chunk_iotas()
    incl = r >= c
    ones_col = (lax.broadcasted_iota(jnp.int32, (CHUNK, 128), 1) == 0).astype(BF16)

    def col(hd, kind, n, width):
        return _col(cols, (hd * _MLSTM_COLUMN_KINDS + kind) * nc + n, width)

    nw = nw_ref[...]
    s = [s_ref[hd] for hd in range(hp)]
    for grp in range(groups):
        chunks = range(grp * nc // groups, (grp + 1) * nc // groups)
        chains = [(hd, n) for n in chunks for hd in range(hp)]
        q_c = {ch: q[ch[1] * CHUNK:(ch[1] + 1) * CHUNK, ch[0] * HEAD_QK:(ch[0] + 1) * HEAD_QK] for ch in chains}
        k_c = {ch: k[ch[1] * CHUNK:(ch[1] + 1) * CHUNK, ch[0] * HEAD_QK:(ch[0] + 1) * HEAD_QK] for ch in chains}
        v_ext = {ch: jnp.concatenate([v_ref[0, ch[1] * CHUNK:(ch[1] + 1) * CHUNK, ch[0] * HEAD_V:(ch[0] + 1) * HEAD_V],
                                      ones_col], axis=1) for ch in chains}

        qk = {ch: _dot_nt(q_c[ch], k_c[ch]) for ch in chains}
        amat, kw = {}, {}
        for ch in chains:
            hd, n = ch
            arg = col(hd, 0, n, CHUNK) + d_rows[hd][n:n + 1, :]
            amat[ch] = (qk[ch] * jnp.exp(jnp.where(incl, arg, -jnp.inf))).astype(BF16)
            kw[ch] = (k_c[ch] * col(hd, 3, n, HEAD_QK)).astype(BF16)
        av = {ch: _dot(amat[ch], v_ext[ch]) for ch in chains}
        kv = {ch: _dot_tn(kw[ch], v_ext[ch]) for ch in chains}

        s_in = {}
        for hd in range(hp):
            for n in chunks:
                s_in[(hd, n)] = s[hd]
                s[hd] = w_state[hd][n:n + 1, :] * s[hd] + kv[(hd, n)]
        qs = {ch: _dot(q_c[ch], s_in[ch]) for ch in chains}

        for ch in chains:
            hd, n = ch
            lo = n * CHUNK
            num_ext = col(hd, 1, n, HEAD_V + 128) * qs[ch] + av[ch]
            denom = jnp.maximum(jnp.abs(num_ext[:, HEAD_V:HEAD_V + 1]), col(hd, 2, n, 1))
            hh = num_ext[:, :HEAD_V] / denom
            gate = (og_ref[0, lo:lo + CHUNK, hd * HEAD_V:(hd + 1) * HEAD_V].astype(F32)
                    * z_ref[0, lo:lo + CHUNK, hd * HEAD_V:(hd + 1) * HEAD_V].astype(F32))
            o_ref[0, lo:lo + CHUNK, hd * HEAD_V:(hd + 1) * HEAD_V] = (_rms(hh, nw) * gate).astype(o_ref.dtype)
    for hd in range(hp):
        s_ref[hd] = s[hd]


def _mlstm_recurrence(p3, gates, i_bias, f_bias, norm_w, groups=MLSTM_CHUNK_GROUPS):
    b, t, _ = p3.shape
    tb, hp = TIME_BLOCK, MLSTM_HEADS_PER_STEP
    wq, wv = hp * HEAD_QK, hp * HEAD_V
    nk = QK_W // wq
    nv = 2 * QK_W // wv
    nh = D_INNER // wv
    smem = pl.BlockSpec(memory_space=pltpu.SMEM)
    gate_spec = functools.partial(_gate_spec, hp, t // tb - 1)
    return pl.pallas_call(
        functools.partial(_mlstm_kernel, groups=groups),
        grid=(b, N_HEADS // hp, t // tb),
        in_specs=[
            smem, smem,
            pl.BlockSpec((1, tb, wq), lambda bi, h, ti: (bi, ti, h)),
            pl.BlockSpec((1, tb, wq), lambda bi, h, ti: (bi, ti, nk + h)),
            pl.BlockSpec((1, tb, wv), lambda bi, h, ti: (bi, ti, nv + h)),
            pl.BlockSpec((1, tb, wv), lambda bi, h, ti: (bi, ti, nv + nh + h)),
            pl.BlockSpec((1, tb, wv), lambda bi, h, ti: (bi, ti, nv + 2 * nh + h)),
            gate_spec(0, 0), gate_spec(1, 0), gate_spec(0, 1), gate_spec(1, 1),
            pl.BlockSpec((1, HEAD_V), lambda bi, h, ti: (0, 0)),
        ],
        out_specs=pl.BlockSpec((1, tb, wv), lambda bi, h, ti: (bi, ti, h)),
        out_shape=jax.ShapeDtypeStruct((b, t, D_INNER), BF16),
        scratch_shapes=[
            pltpu.VMEM((hp, HEAD_QK, HEAD_V + 128), F32),
            pltpu.VMEM((hp, 8, 128), F32),
            pltpu.VMEM((CHUNK, 128), F32),
            pltpu.VMEM((hp, CHUNKS_PER_BLOCK, CHUNK), F32),
            pltpu.VMEM((hp, CHUNKS_PER_BLOCK, 128), F32),
        ],
        compiler_params=pltpu.CompilerParams(
            dimension_semantics=("parallel", "parallel", "arbitrary"),
            vmem_limit_bytes=VMEM_LIMIT_BYTES),
        name="mlstm_recurrence",
    )(i_bias, f_bias, p3, p3, p3, p3, p3, gates, gates, gates, gates, norm_w.reshape(1, HEAD_V))


def _tile_modes(*widths_and_modes):
    ranges, lo = [], 0
    for width, mode in widths_and_modes:
        ranges.append((lo, lo + width // COL_TILE, mode))
        lo += width // COL_TILE
    return tuple(ranges)


_GDN_TILE_MODES = _tile_modes((2 * QK_W + D_INNER, "conv_silu"), (D_INNER, "silu"))
_MLSTM_TILE_MODES = _tile_modes((2 * QK_W, "conv_silu"), (D_INNER, "none"), (D_INNER, "sigmoid"),
                                (D_INNER, "silu"))


def _gate_weight_t(w_in_t, layer):
    n_main = w_in_t.shape[1] - 2 * N_HEADS
    return jnp.pad(w_in_t[layer, n_main:, :], ((0, GATE_PAD - 2 * N_HEADS), (0, 0)))


def _gate_rows(pg_t, b, t):
    return pg_t[:2 * N_HEADS].reshape(2, N_HEADS, b, t // CHUNK, CHUNK)


def kernel(x, norm_w, final_norm_w, gdn_w_in, gdn_conv_w, gdn_a_log, gdn_dt_bias, gdn_norm_w, gdn_w_out,
           mlstm_w_in, mlstm_conv_w, mlstm_i_bias, mlstm_f_bias, mlstm_norm_w, mlstm_w_out):
    b, t, d = x.shape
    depth = norm_w.shape[0]
    x2d = x.reshape(b * t, d)
    w_in_t = (jnp.swapaxes(gdn_w_in, 1, 2), jnp.swapaxes(mlstm_w_in, 1, 2))

    inproj_arms = [dict(sub_rows=128), dict(sub_rows=128), dict(sub_rows=1024), dict(sub_rows=512)]
    outproj_arms = [dict(sub_rows=512), dict(sub_rows=256, row_tile=256), dict(sub_rows=512)]
    gdn_heads = {0: 4, 2: 2}
    mlstm_groups = {1: 4, 3: 2}

    h, pg = _prenorm(x2d, norm_w[0], _gate_weight_t(w_in_t[0], 0))
    for i in range(depth):
        j = i // 2
        gates = _gate_rows(pg, b, t)
        if i % 2 == 0:
            p = _inproj(h, w_in_t[0], j, gdn_conv_w[j], _GDN_TILE_MODES, t, **inproj_arms[i]).reshape(b, t, -1)
            g = _gdn_recurrence(p, gates, gdn_a_log[j], gdn_dt_bias[j], gdn_norm_w[j], hp=gdn_heads[i])
            w_out = gdn_w_out
        else:
            p = _inproj(h, w_in_t[1], j, mlstm_conv_w[j], _MLSTM_TILE_MODES, t, **inproj_arms[i]).reshape(b, t, -1)
            g = _mlstm_recurrence(p, gates, mlstm_i_bias[j], mlstm_f_bias[j], mlstm_norm_w[j],
                                  groups=mlstm_groups[i])
            w_out = mlstm_w_out
        g = g.reshape(b * t, D_INNER)
        if i == depth - 1:
            x2d = _outproj_final(g, w_out, j, x2d, final_norm_w)
        else:
            x2d, h, pg = _outproj(g, w_out, j, x2d, norm_w[i + 1], _gate_weight_t(w_in_t[(i + 1) % 2], (i + 1) // 2),
                                  **outproj_arms[i])
    return x2d.reshape(b, t, d)
```

```python
import functools

import jax
import jax.numpy as jnp
from jax import lax
from jax.experimental import pallas as pl
from jax.experimental.pallas import tpu as pltpu

F32 = jnp.float32
BF16 = jnp.bfloat16
HIGHEST = lax.Precision.HIGHEST

D_MODEL = 1024
N_HEADS = 8
HEAD_QK = 128
HEAD_V = 256
D_INNER = N_HEADS * HEAD_V
QK_W = N_HEADS * HEAD_QK
CONV_K = 4
CHUNK = 64
NORM_EPS = 1e-6
GATE_PAD = 128

INPROJ_ROW_TILE = 1024
ROW_TILE = 512
COL_TILE = 1024
TIME_BLOCK = 512
CHUNKS_PER_BLOCK = TIME_BLOCK // CHUNK
CONV_HALO = 8
VMEM_LIMIT_BYTES = 48 * 1024 * 1024


def _dot(a, b):
    return jnp.dot(a.astype(BF16), b.astype(BF16), preferred_element_type=F32)


def _dot_nt(a, b):
    return lax.dot_general(a.astype(BF16), b.astype(BF16), (((1,), (1,)), ((), ())),
                           preferred_element_type=F32)


def _dot_tn(a, b):
    return lax.dot_general(a.astype(BF16), b.astype(BF16), (((0,), (0,)), ((), ())),
                           preferred_element_type=F32)


def _dot_f32(a, b):
    return jnp.dot(a, b, precision=HIGHEST, preferred_element_type=F32)


def _sigmoid(x):
    return jax.nn.sigmoid(x)


def _silu(x):
    return x * _sigmoid(x)


def _softplus(x):
    return jnp.maximum(x, 0.0) + jnp.log1p(jnp.exp(-jnp.abs(x)))


def _rms(x, w):
    return x * lax.rsqrt(jnp.mean(x * x, axis=-1, keepdims=True) + NORM_EPS) * w


def _gates_t(wg_ref, h):
    return lax.dot_general(wg_ref[...].astype(BF16), h, (((1,), (1,)), ((), ())), preferred_element_type=F32)


def _prenorm_kernel(x_ref, nw_ref, wg_ref, h_ref, pg_ref):
    h = _rms(x_ref[...], nw_ref[...]).astype(BF16)
    h_ref[...] = h
    pg_ref[...] = _gates_t(wg_ref, h)


def _prenorm(x2d, nw, w_gate):
    m, d = x2d.shape
    return pl.pallas_call(
        _prenorm_kernel,
        grid=(m // ROW_TILE,),
        in_specs=[
            pl.BlockSpec((ROW_TILE, d), lambda i: (i, 0)),
            pl.BlockSpec((1, d), lambda i: (0, 0)),
            pl.BlockSpec((GATE_PAD, d), lambda i: (0, 0)),
        ],
        out_specs=[
            pl.BlockSpec((ROW_TILE, d), lambda i: (i, 0)),
            pl.BlockSpec((GATE_PAD, ROW_TILE), lambda i: (0, i)),
        ],
        out_shape=[jax.ShapeDtypeStruct((m, d), BF16), jax.ShapeDtypeStruct((GATE_PAD, m), F32)],
        compiler_params=pltpu.CompilerParams(
            dimension_semantics=("parallel",), vmem_limit_bytes=VMEM_LIMIT_BYTES),
        name="prenorm",
    )(x2d, nw.reshape(1, d), w_gate)


def _conv_taps(tail, y, cw):
    rows = y.shape[0]
    ext = jnp.concatenate([tail, y], axis=0)
    acc = cw[CONV_K - 1:CONV_K, :] * y
    for j in range(CONV_K - 1):
        lo = CONV_HALO - (CONV_K - 1) + j
        acc = acc + cw[j:j + 1, :] * ext[lo:lo + rows, :]
    return acc


_ACTIVATIONS = {"none": lambda y: y, "silu": _silu, "sigmoid": _sigmoid}


def _inproj_kernel(h_ref, w_ref, cw_ref, p_ref, wb_ref, tail_ref, *, tile_modes, row_tiles_per_seq):
    j, i = pl.program_id(0), pl.program_id(1)

    @pl.when(i == 0)
    def _():
        for c0 in range(0, COL_TILE, 128):
            wb_ref[:, c0:c0 + 128] = w_ref[c0:c0 + 128, :].T.astype(BF16)

    for lo, hi, mode in tile_modes:
        @pl.when((j >= lo) & (j < hi))
        def _(mode=mode):
            y = jnp.dot(h_ref[...], wb_ref[...], preferred_element_type=F32)
            if mode != "conv_silu":
                p_ref[...] = _ACTIVATIONS[mode](y).astype(p_ref.dtype)
                return

            @pl.when(i % row_tiles_per_seq == 0)
            def _():
                tail_ref[...] = jnp.zeros_like(tail_ref)

            p_ref[...] = _silu(_conv_taps(tail_ref[...], y, cw_ref[...])).astype(p_ref.dtype)
            tail_ref[...] = y[y.shape[0] - CONV_HALO:, :]


def _inproj(h2d, w_in_t, layer, conv_w, tile_modes, seq_len, row_tile=INPROJ_ROW_TILE):
    m, d = h2d.shape
    n_main = tile_modes[-1][1] * COL_TILE
    n_conv = conv_w.shape[1] // COL_TILE
    body = functools.partial(_inproj_kernel, tile_modes=tile_modes, row_tiles_per_seq=seq_len // row_tile)
    return pl.pallas_call(
        body,
        grid=(n_main // COL_TILE, m // row_tile),
        in_specs=[
            pl.BlockSpec((row_tile, d), lambda j, i: (i, 0)),
            pl.BlockSpec((None, COL_TILE, d), lambda j, i: (layer, j, 0)),
            pl.BlockSpec((CONV_K, COL_TILE), lambda j, i: (0, jnp.minimum(j, n_conv - 1))),
        ],
        out_specs=pl.BlockSpec((row_tile, COL_TILE), lambda j, i: (i, j)),
        out_shape=jax.ShapeDtypeStruct((m, n_main), BF16),
        scratch_shapes=[pltpu.VMEM((d, COL_TILE), BF16),
                        pltpu.VMEM((CONV_HALO, COL_TILE), F32)],
        compiler_params=pltpu.CompilerParams(
            dimension_semantics=("arbitrary", "arbitrary"), vmem_limit_bytes=VMEM_LIMIT_BYTES),
        name="inproj",
    )(h2d, w_in_t, conv_w)


def _outproj_kernel(g_ref, w_ref, x_ref, nw_ref, wg_ref, xo_ref, h_ref, pg_ref, wb_ref):
    @pl.when(pl.program_id(0) == 0)
    def _():
        wb_ref[...] = w_ref[...].astype(BF16)

    y = x_ref[...] + jnp.dot(g_ref[...], wb_ref[...], preferred_element_type=F32)
    xo_ref[...] = y
    h = _rms(y, nw_ref[...]).astype(BF16)
    h_ref[...] = h
    pg_ref[...] = _gates_t(wg_ref, h)


def _outproj_final_kernel(g_ref, w_ref, x_ref, fw_ref, o_ref, wb_ref):
    @pl.when(pl.program_id(0) == 0)
    def _():
        wb_ref[...] = w_ref[...].astype(BF16)

    y = x_ref[...] + jnp.dot(g_ref[...], wb_ref[...], preferred_element_type=F32)
    o_ref[...] = _rms(y, fw_ref[...])


def _outproj(g2d, w_out, layer, x2d, nw_next, w_gate_next):
    m, d = x2d.shape
    k = g2d.shape[1]
    row = lambda i: (i, 0)
    fixed = lambda i: (0, 0)
    return pl.pallas_call(
        _outproj_kernel,
        grid=(m // ROW_TILE,),
        in_specs=[
            pl.BlockSpec((ROW_TILE, k), row),
            pl.BlockSpec((None, k, d), lambda i: (layer, 0, 0)),
            pl.BlockSpec((ROW_TILE, d), row),
            pl.BlockSpec((1, d), fixed),
            pl.BlockSpec((GATE_PAD, d), fixed),
        ],
        out_specs=[
            pl.BlockSpec((ROW_TILE, d), row),
            pl.BlockSpec((ROW_TILE, d), row),
            pl.BlockSpec((GATE_PAD, ROW_TILE), lambda i: (0, i)),
        ],
        out_shape=[jax.ShapeDtypeStruct((m, d), F32), jax.ShapeDtypeStruct((m, d), BF16),
                   jax.ShapeDtypeStruct((GATE_PAD, m), F32)],
        scratch_shapes=[pltpu.VMEM((k, d), BF16)],
        compiler_params=pltpu.CompilerParams(
            dimension_semantics=("arbitrary",), vmem_limit_bytes=VMEM_LIMIT_BYTES),
        name="outproj",
    )(g2d, w_out, x2d, nw_next.reshape(1, d), w_gate_next)


def _outproj_final(g2d, w_out, layer, x2d, final_w):
    m, d = x2d.shape
    k = g2d.shape[1]
    row = lambda i: (i, 0)
    return pl.pallas_call(
        _outproj_final_kernel,
        grid=(m // ROW_TILE,),
        in_specs=[
            pl.BlockSpec((ROW_TILE, k), row),
            pl.BlockSpec((None, k, d), lambda i: (layer, 0, 0)),
            pl.BlockSpec((ROW_TILE, d), row),
            pl.BlockSpec((1, d), lambda i: (0, 0)),
        ],
        out_specs=pl.BlockSpec((ROW_TILE, d), row),
        out_shape=jax.ShapeDtypeStruct((m, d), F32),
        scratch_shapes=[pltpu.VMEM((k, d), BF16)],
        compiler_params=pltpu.CompilerParams(
            dimension_semantics=("arbitrary",), vmem_limit_bytes=VMEM_LIMIT_BYTES),
        name="outproj_final",
    )(g2d, w_out, x2d, final_w.reshape(1, d))


def _chunk_iotas():
    r = lax.broadcasted_iota(jnp.int32, (CHUNK, CHUNK), 0)
    c = lax.broadcasted_iota(jnp.int32, (CHUNK, CHUNK), 1)
    return r, c


def _gate_spec(heads, last_block, kind, ahead):
    return pl.BlockSpec((None, heads, None, CHUNKS_PER_BLOCK, CHUNK),
                        lambda bi, h, ti: (kind, h, bi, jnp.minimum(ti + ahead, last_block), 0))


def _col(xt, n, width):
    return jnp.broadcast_to(xt[:, n:n + 1], (CHUNK, width))


def _columns(rows_list):
    rows = jnp.concatenate(rows_list, axis=0)
    if rows.shape[0] < 128:
        rows = jnp.concatenate([rows, jnp.zeros((128 - rows.shape[0], CHUNK), F32)], axis=0)
    rows = jnp.concatenate([rows, jnp.zeros((128, 128 - CHUNK), F32)], axis=1)
    return rows.T[:CHUNK, :]


def _unit_lower_inverse_minus_identity(a_list, r, c):
    base = 8
    blk = (r // base) == (c // base)
    a0 = [jnp.where(blk, a, 0.0) for a in a_list]
    a2 = [_dot(x, x) for x in a0]
    a34 = [_dot(jnp.concatenate([x, y], axis=0), y) for x, y in zip(a0, a2)]
    p = [y - x - z[:CHUNK] for x, y, z in zip(a0, a2, a34)]
    a4 = [z[CHUNK:] for z in a34]
    n = [x + y + _dot(x, y) for x, y in zip(p, a4)]
    s = base
    while s < CHUNK:
        off_mask = ((r // (2 * s)) == (c // (2 * s))) & (((r // s) % 2) == 1) & (((c // s) % 2) == 0)
        off = [jnp.where(off_mask, a, 0.0) for a in a_list]
        x = [o + _dot(o, m) for o, m in zip(off, n)]
        n = [m - y - _dot(m, y) for m, y in zip(n, x)]
        s *= 2
    return n


GDN_HEADS_PER_STEP = 4
_GDN_COLUMN_KINDS = 5


def _gdn_kernel(alog_ref, dtb_ref, q_ref, k_ref, v_ref, z_ref, a_ref, bt_ref, nw_ref, o_ref, s_ref):
    hp, nc = GDN_HEADS_PER_STEP, CHUNKS_PER_BLOCK
    hg = pl.program_id(1)

    @pl.when(pl.program_id(2) == 0)
    def _():
        s_ref[...] = jnp.zeros_like(s_ref)

    q = q_ref[0].astype(F32)
    k = k_ref[0].astype(F32)
    v = v_ref[0].astype(F32)

    r, c = _chunk_iotas()
    incl = r >= c
    strict = r > c
    upper = (r <= c).astype(F32)

    qh, kh, vh, gcs, es, rows = [], [], [], [], [], []
    for hd in range(hp):
        h = hg * hp + hd
        qs = q[:, hd * HEAD_QK:(hd + 1) * HEAD_QK]
        ks = k[:, hd * HEAD_QK:(hd + 1) * HEAD_QK]
        qh.append(qs * lax.rsqrt(jnp.sum(qs * qs, axis=-1, keepdims=True) + 1e-6) * (HEAD_QK ** -0.5))
        kh.append(ks * lax.rsqrt(jnp.sum(ks * ks, axis=-1, keepdims=True) + 1e-6))
        vh.append(v[:, hd * HEAD_V:(hd + 1) * HEAD_V])
        neg_rate = -jnp.exp(jnp.full((1, 1), alog_ref[h], F32))
        g = neg_rate * _softplus(a_ref[hd] + dtb_ref[h])
        beta = _sigmoid(bt_ref[hd])
        gc = _dot_f32(g, upper)
        e = jnp.exp(gc)
        gcs.append(gc)
        es.append(e)
        rows += [gc, beta, beta * e, e, jnp.exp(gc[:, CHUNK - 1:CHUNK] - gc)]
    heads_per_transpose = 128 // (_GDN_COLUMN_KINDS * nc)
    cols = [_columns(rows[i * _GDN_COLUMN_KINDS:(i + heads_per_transpose) * _GDN_COLUMN_KINDS])
            for i in range(0, hp, heads_per_transpose)]

    def col(hd, kind, n, width):
        i = ((hd % heads_per_transpose) * _GDN_COLUMN_KINDS + kind) * nc + n
        return _col(cols[hd // heads_per_transpose], i, width)

    chains = [(hd, n) for n in range(nc) for hd in range(hp)]
    q_c = {ch: qh[ch[0]][ch[1] * CHUNK:(ch[1] + 1) * CHUNK] for ch in chains}
    k_c = {ch: kh[ch[0]][ch[1] * CHUNK:(ch[1] + 1) * CHUNK] for ch in chains}
    v_c = {ch: vh[ch[0]][ch[1] * CHUNK:(ch[1] + 1) * CHUNK] for ch in chains}

    kq = {ch: _dot_nt(jnp.concatenate([k_c[ch], q_c[ch]], axis=0), k_c[ch]) for ch in chains}
    a_list, attn = [], {}
    for ch in chains:
        hd, n = ch
        gc_row = gcs[hd][n:n + 1, :]
        decay = jnp.where(incl, jnp.exp(jnp.where(incl, col(hd, 0, n, CHUNK) - gc_row, 0.0)), 0.0)
        a_list.append(jnp.where(strict, col(hd, 1, n, CHUNK) * kq[ch][:CHUNK] * decay, 0.0))
        attn[ch] = (kq[ch][CHUNK:] * decay).astype(BF16)
    n_list = _unit_lower_inverse_minus_identity(a_list, r, c)
    uw, kd = {}, {}
    for ch, nmat in zip(chains, n_list):
        hd, n = ch
        rhs = jnp.concatenate([col(hd, 1, n, HEAD_V) * v_c[ch], col(hd, 2, n, HEAD_QK) * k_c[ch]], axis=1)
        uw[ch] = (rhs + _dot(nmat, rhs)).astype(BF16)
        kd[ch] = (k_c[ch] * col(hd, 4, n, HEAD_QK)).astype(BF16)
    kd_uw = {ch: _dot_tn(kd[ch], uw[ch]) for ch in chains}
    at_uw = {ch: _dot(attn[ch], uw[ch]) for ch in chains}
    lhs = {}
    for ch in chains:
        hd, n = ch
        to_out = q_c[ch] * col(hd, 3, n, HEAD_QK) - at_uw[ch][:, HEAD_V:]
        lhs[ch] = jnp.concatenate([-kd_uw[ch][:, HEAD_V:], to_out], axis=0).astype(BF16)

    nw = nw_ref[...]
    s = [s_ref[hd] for hd in range(hp)]
    for n in range(nc):
        lo = n * CHUNK
        m = [_dot(lhs[(hd, n)], s[hd]) for hd in range(hp)]
        o = [m[hd][HEAD_QK:] + at_uw[(hd, n)][:, :HEAD_V] for hd in range(hp)]
        s = [es[hd][n:n + 1, CHUNK - 1:CHUNK] * s[hd] + m[hd][:HEAD_QK] + kd_uw[(hd, n)][:, :HEAD_V]
             for hd in range(hp)]
        for hd in range(hp):
            z = z_ref[0, lo:lo + CHUNK, hd * HEAD_V:(hd + 1) * HEAD_V]
            o_ref[0, lo:lo + CHUNK, hd * HEAD_V:(hd + 1) * HEAD_V] = (_rms(o[hd], nw) * z.astype(F32)).astype(o_ref.dtype)
    for hd in range(hp):
        s_ref[hd] = s[hd]


def _gdn_recurrence(p3, gates, a_log, dt_bias, norm_w):
    b, t, _ = p3.shape
    tb, hp = TIME_BLOCK, GDN_HEADS_PER_STEP
    wq, wv = hp * HEAD_QK, hp * HEAD_V
    nk = QK_W // wq
    nv = 2 * QK_W // wv
    nz = (2 * QK_W + D_INNER) // wv
    smem = pl.BlockSpec(memory_space=pltpu.SMEM)
    gate_spec = functools.partial(_gate_spec, hp, t // tb - 1)
    return pl.pallas_call(
        _gdn_kernel,
        grid=(b, N_HEADS // hp, t // tb),
        in_specs=[
            smem, smem,
            pl.BlockSpec((1, tb, wq), lambda bi, h, ti: (bi, ti, h)),
            pl.BlockSpec((1, tb, wq), lambda bi, h, ti: (bi, ti, nk + h)),
            pl.BlockSpec((1, tb, wv), lambda bi, h, ti: (bi, ti, nv + h)),
            pl.BlockSpec((1, tb, wv), lambda bi, h, ti: (bi, ti, nz + h)),
            gate_spec(0, 0), gate_spec(1, 0),
            pl.BlockSpec((1, HEAD_V), lambda bi, h, ti: (0, 0)),
        ],
        out_specs=pl.BlockSpec((1, tb, wv), lambda bi, h, ti: (bi, ti, h)),
        out_shape=jax.ShapeDtypeStruct((b, t, D_INNER), BF16),
        scratch_shapes=[pltpu.VMEM((hp, HEAD_QK, HEAD_V), F32)],
        compiler_params=pltpu.CompilerParams(
            dimension_semantics=("parallel", "parallel", "arbitrary"),
            vmem_limit_bytes=VMEM_LIMIT_BYTES),
        name="gdn_recurrence",
    )(a_log, dt_bias, p3, p3, p3, p3, gates, gates, norm_w.reshape(1, HEAD_V))


MLSTM_HEADS_PER_STEP = 2
_MLSTM_COLUMN_KINDS = 4


def _prefix_max_lanes(x):
    y = jnp.concatenate([x, jnp.full((x.shape[0], 128 - CHUNK), -jnp.inf, F32)], axis=1)
    s = 1
    while s < CHUNK:
        y = jnp.maximum(y, pltpu.roll(y, s, axis=1))
        s *= 2
    return y[:, :CHUNK]


MLSTM_CHUNK_GROUPS = 4


def _mlstm_gate_setup(ig_ref, fg_ref, ib_ref, fb_ref, hg, m_ref, cols_ref, d_ref, ws_ref):
    hp, nc = MLSTM_HEADS_PER_STEP, CHUNKS_PER_BLOCK
    r, c = _chunk_iotas()
    upper = (r <= c).astype(F32)
    chunk_id = lax.broadcasted_iota(jnp.int32, (nc, 1), 0)
    rows = []
    for hd in range(hp):
        h = hg * hp + hd
        i_pre = ig_ref[hd] + ib_ref[h]
        lf = -_softplus(-(fg_ref[hd] + fb_ref[h]))
        bc = _dot_f32(lf, upper)
        b_last = bc[:, CHUNK - 1:CHUNK]
        d = i_pre - bc
        g_end = b_last + d
        g_max = jnp.max(g_end, axis=-1, keepdims=True)
        r_max = bc + _prefix_max_lanes(d)
        m = m_ref[hd, 0:1, 0:1]
        m_before = jnp.zeros((nc, 1), F32)
        m_after = jnp.zeros((nc, 1), F32)
        for n in range(nc):
            m_before = jnp.where(chunk_id == n, m, m_before)
            m = jnp.maximum(b_last[n:n + 1, :] + m, g_max[n:n + 1, :])
            m_after = jnp.where(chunk_id == n, m, m_after)
        m_ref[hd] = jnp.broadcast_to(m, m_ref.shape[1:])
        m_t = jnp.maximum(bc + m_before, r_max)
        d_ref[hd] = d
        ws_ref[hd] = jnp.broadcast_to(jnp.exp(b_last + m_before - m_after), ws_ref.shape[1:])
        rows += [bc - m_t, jnp.exp(bc + m_before - m_t), jnp.exp(-m_t), jnp.exp(g_end - m_after)]
    cols_ref[...] = _columns(rows)


def _mlstm_kernel(ib_ref, fb_ref, q_ref, k_ref, v_ref, og_ref, z_ref, ig_ref, fg_ref, ig_next_ref, fg_next_ref,
                  nw_ref, o_ref, s_ref, m_ref, cols_ref, d_ref, ws_ref):
    hp, nc = MLSTM_HEADS_PER_STEP, CHUNKS_PER_BLOCK
    hg = pl.program_id(1)
    setup = functools.partial(_mlstm_gate_setup, ib_ref=ib_ref, fb_ref=fb_ref, hg=hg, m_ref=m_ref,
                              cols_ref=cols_ref, d_ref=d_ref, ws_ref=ws_ref)

    @pl.when(pl.program_id(2) == 0)
    def _():
        s_ref[...] = jnp.zeros_like(s_ref)
        m_ref[...] = jnp.zeros_like(m_ref)
        setup(ig_ref, fg_ref)

    cols = cols_ref[...]
    d_rows = [d_ref[hd] for hd in range(hp)]
    w_state = [ws_ref[hd][:, 0:1] for hd in range(hp)]
    setup(ig_next_ref, fg_next_ref)

    q = q_ref[0].astype(F32)
    k = k_ref[0].astype(F32) * (HEAD_QK ** -0.5)

    r, c = _chunk_iotas()
    incl = r >= c
    ones_col = (lax.broadcasted_iota(jnp.int32, (CHUNK, 128), 1) == 0).astype(BF16)

    def col(hd, kind, n, width):
        return _col(cols, (hd * _MLSTM_COLUMN_KINDS + kind) * nc + n, width)

    nw = nw_ref[...]
    s = [s_ref[hd] for hd in range(hp)]
    for grp in range(MLSTM_CHUNK_GROUPS):
        chunks = range(grp * nc // MLSTM_CHUNK_GROUPS, (grp + 1) * nc // MLSTM_CHUNK_GROUPS)
        chains = [(hd, n) for n in chunks for hd in range(hp)]
        q_c = {ch: q[ch[1] * CHUNK:(ch[1] + 1) * CHUNK, ch[0] * HEAD_QK:(ch[0] + 1) * HEAD_QK] for ch in chains}
        k_c = {ch: k[ch[1] * CHUNK:(ch[1] + 1) * CHUNK, ch[0] * HEAD_QK:(ch[0] + 1) * HEAD_QK] for ch in chains}
        v_ext = {ch: jnp.concatenate([v_ref[0, ch[1] * CHUNK:(ch[1] + 1) * CHUNK, ch[0] * HEAD_V:(ch[0] + 1) * HEAD_V],
                                      ones_col], axis=1) for ch in chains}

        qk = {ch: _dot_nt(q_c[ch], k_c[ch]) for ch in chains}
        amat, kw = {}, {}
        for ch in chains:
            hd, n = ch
            arg = col(hd, 0, n, CHUNK) + d_rows[hd][n:n + 1, :]
            amat[ch] = (qk[ch] * jnp.exp(jnp.where(incl, arg, -jnp.inf))).astype(BF16)
            kw[ch] = (k_c[ch] * col(hd, 3, n, HEAD_QK)).astype(BF16)
        av = {ch: _dot(amat[ch], v_ext[ch]) for ch in chains}
        kv = {ch: _dot_tn(kw[ch], v_ext[ch]) for ch in chains}

        s_in = {}
        for hd in range(hp):
            for n in chunks:
                s_in[(hd, n)] = s[hd]
                s[hd] = w_state[hd][n:n + 1, :] * s[hd] + kv[(hd, n)]
        qs = {ch: _dot(q_c[ch], s_in[ch]) for ch in chains}

        for ch in chains:
            hd, n = ch
            lo = n * CHUNK
            num_ext = col(hd, 1, n, HEAD_V + 128) * qs[ch] + av[ch]
            denom = jnp.maximum(jnp.abs(num_ext[:, HEAD_V:HEAD_V + 1]), col(hd, 2, n, 1))
            hh = num_ext[:, :HEAD_V] / denom
            gate = (og_ref[0, lo:lo + CHUNK, hd * HEAD_V:(hd + 1) * HEAD_V].astype(F32)
                    * z_ref[0, lo:lo + CHUNK, hd * HEAD_V:(hd + 1) * HEAD_V].astype(F32))
            o_ref[0, lo:lo + CHUNK, hd * HEAD_V:(hd + 1) * HEAD_V] = (_rms(hh, nw) * gate).astype(o_ref.dtype)
    for hd in range(hp):
        s_ref[hd] = s[hd]


def _mlstm_recurrence(p3, gates, i_bias, f_bias, norm_w):
    b, t, _ = p3.shape
    tb, hp = TIME_BLOCK, MLSTM_HEADS_PER_STEP
    wq, wv = hp * HEAD_QK, hp * HEAD_V
    nk = QK_W // wq
    nv = 2 * QK_W // wv
    nh = D_INNER // wv
    smem = pl.BlockSpec(memory_space=pltpu.SMEM)
    gate_spec = functools.partial(_gate_spec, hp, t // tb - 1)
    return pl.pallas_call(
        _mlstm_kernel,
        grid=(b, N_HEADS // hp, t // tb),
        in_specs=[
            smem, smem,
            pl.BlockSpec((1, tb, wq), lambda bi, h, ti: (bi, ti, h)),
            pl.BlockSpec((1, tb, wq), lambda bi, h, ti: (bi, ti, nk + h)),
            pl.BlockSpec((1, tb, wv), lambda bi, h, ti: (bi, ti, nv + h)),
            pl.BlockSpec((1, tb, wv), lambda bi, h, ti: (bi, ti, nv + nh + h)),
            pl.BlockSpec((1, tb, wv), lambda bi, h, ti: (bi, ti, nv + 2 * nh + h)),
            gate_spec(0, 0), gate_spec(1, 0), gate_spec(0, 1), gate_spec(1, 1),
            pl.BlockSpec((1, HEAD_V), lambda bi, h, ti: (0, 0)),
        ],
        out_specs=pl.BlockSpec((1, tb, wv), lambda bi, h, ti: (bi, ti, h)),
        out_shape=jax.ShapeDtypeStruct((b, t, D_INNER), BF16),
        scratch_shapes=[
            pltpu.VMEM((hp, HEAD_QK, HEAD_V + 128), F32),
            pltpu.VMEM((hp, 8, 128), F32),
            pltpu.VMEM((CHUNK, 128), F32),
            pltpu.VMEM((hp, CHUNKS_PER_BLOCK, CHUNK), F32),
            pltpu.VMEM((hp, CHUNKS_PER_BLOCK, 128), F32),
        ],
        compiler_params=pltpu.CompilerParams(
            dimension_semantics=("parallel", "parallel", "arbitrary"),
            vmem_limit_bytes=VMEM_LIMIT_BYTES),
        name="mlstm_recurrence",
    )(i_bias, f_bias, p3, p3, p3, p3, p3, gates, gates, gates, gates, norm_w.reshape(1, HEAD_V))


def _tile_modes(*widths_and_modes):
    ranges, lo = [], 0
    for width, mode in widths_and_modes:
        ranges.append((lo, lo + width // COL_TILE, mode))
        lo += width // COL_TILE
    return tuple(ranges)


_GDN_TILE_MODES = _tile_modes((2 * QK_W + D_INNER, "conv_silu"), (D_INNER, "silu"))
_MLSTM_TILE_MODES = _tile_modes((2 * QK_W, "conv_silu"), (D_INNER, "none"), (D_INNER, "sigmoid"),
                                (D_INNER, "silu"))


def _gate_weight_t(w_in_t, layer):
    n_main = w_in_t.shape[1] - 2 * N_HEADS
    return jnp.pad(w_in_t[layer, n_main:, :], ((0, GATE_PAD - 2 * N_HEADS), (0, 0)))


def _gate_rows(pg_t, b, t):
    return pg_t[:2 * N_HEADS].reshape(2, N_HEADS, b, t // CHUNK, CHUNK)


def kernel(x, norm_w, final_norm_w, gdn_w_in, gdn_conv_w, gdn_a_log, gdn_dt_bias, gdn_norm_w, gdn_w_out,
           mlstm_w_in, mlstm_conv_w, mlstm_i_bias, mlstm_f_bias, mlstm_norm_w, mlstm_w_out):
    b, t, d = x.shape
    depth = norm_w.shape[0]
    x2d = x.reshape(b * t, d)
    w_in_t = (jnp.swapaxes(gdn_w_in, 1, 2), jnp.swapaxes(mlstm_w_in, 1, 2))

    inproj_row_tile = [1024, 1024, 2048, 2048]

    h, pg = _prenorm(x2d, norm_w[0], _gate_weight_t(w_in_t[0], 0))
    for i in range(depth):
        j = i // 2
        gates = _gate_rows(pg, b, t)
        if i % 2 == 0:
            p = _inproj(h, w_in_t[0], j, gdn_conv_w[j], _GDN_TILE_MODES, t, inproj_row_tile[i]).reshape(b, t, -1)
            g = _gdn_recurrence(p, gates, gdn_a_log[j], gdn_dt_bias[j], gdn_norm_w[j])
            w_out = gdn_w_out
        else:
            p = _inproj(h, w_in_t[1], j, mlstm_conv_w[j], _MLSTM_TILE_MODES, t, inproj_row_tile[i]).reshape(b, t, -1)
            g = _mlstm_recurrence(p, gates, mlstm_i_bias[j], mlstm_f_bias[j], mlstm_norm_w[j])
            w_out = mlstm_w_out
        g = g.reshape(b * t, D_INNER)
        if i == depth - 1:
            x2d = _outproj_final(g, w_out, j, x2d, final_norm_w)
        else:
            x2d, h, pg = _outproj(g, w_out, j, x2d, norm_w[i + 1], _gate_weight_t(w_in_t[(i + 1) % 2], (i + 1) // 2))
    return x2d.reshape(b, t, d)
```

```python
import functools

import jax
import jax.numpy as jnp
from jax import lax
from jax.experimental import pallas as pl
from jax.experimental.pallas import tpu as pltpu

F32 = jnp.float32
BF16 = jnp.bfloat16
HIGHEST = lax.Precision.HIGHEST

D_MODEL = 1024
N_HEADS = 8
HEAD_QK = 128
HEAD_V = 256
D_INNER = N_HEADS * HEAD_V
QK_W = N_HEADS * HEAD_QK
CONV_K = 4
CHUNK = 64
NORM_EPS = 1e-6
GATE_PAD = 128

INPROJ_ROW_TILE = 1024
ROW_TILE = 512
OUTPROJ_ROW_TILE = 1024
OUTPROJ_VMEM_LIMIT_BYTES = 56 * 1024 * 1024
COL_TILE = 1024
TIME_BLOCK = 512
CHUNKS_PER_BLOCK = TIME_BLOCK // CHUNK
CONV_HALO = 8
VMEM_LIMIT_BYTES = 48 * 1024 * 1024


def _dot(a, b):
    return jnp.dot(a.astype(BF16), b.astype(BF16), preferred_element_type=F32)


def _dot_nt(a, b):
    return lax.dot_general(a.astype(BF16), b.astype(BF16), (((1,), (1,)), ((), ())),
                           preferred_element_type=F32)


def _dot_tn(a, b):
    return lax.dot_general(a.astype(BF16), b.astype(BF16), (((0,), (0,)), ((), ())),
                           preferred_element_type=F32)


def _dot_f32(a, b):
    return jnp.dot(a, b, precision=HIGHEST, preferred_element_type=F32)


def _sigmoid(x):
    return jax.nn.sigmoid(x)


def _silu(x):
    return x * _sigmoid(x)


def _softplus(x):
    return jnp.maximum(x, 0.0) + jnp.log1p(jnp.exp(-jnp.abs(x)))


def _rms(x, w):
    return x * lax.rsqrt(jnp.mean(x * x, axis=-1, keepdims=True) + NORM_EPS) * w


def _gates_t(wg_ref, h):
    return lax.dot_general(wg_ref[...].astype(BF16), h, (((1,), (1,)), ((), ())), preferred_element_type=F32)


def _prenorm_kernel(x_ref, nw_ref, wg_ref, h_ref, pg_ref):
    h = _rms(x_ref[...], nw_ref[...]).astype(BF16)
    h_ref[...] = h
    pg_ref[...] = _gates_t(wg_ref, h)


def _prenorm(x2d, nw, w_gate):
    m, d = x2d.shape
    return pl.pallas_call(
        _prenorm_kernel,
        grid=(m // ROW_TILE,),
        in_specs=[
            pl.BlockSpec((ROW_TILE, d), lambda i: (i, 0)),
            pl.BlockSpec((1, d), lambda i: (0, 0)),
            pl.BlockSpec((GATE_PAD, d), lambda i: (0, 0)),
        ],
        out_specs=[
            pl.BlockSpec((ROW_TILE, d), lambda i: (i, 0)),
            pl.BlockSpec((GATE_PAD, ROW_TILE), lambda i: (0, i)),
        ],
        out_shape=[jax.ShapeDtypeStruct((m, d), BF16), jax.ShapeDtypeStruct((GATE_PAD, m), F32)],
        compiler_params=pltpu.CompilerParams(
            dimension_semantics=("parallel",), vmem_limit_bytes=VMEM_LIMIT_BYTES),
        name="prenorm",
    )(x2d, nw.reshape(1, d), w_gate)


def _conv_taps(tail, y, cw):
    rows = y.shape[0]
    ext = jnp.concatenate([tail, y], axis=0)
    acc = cw[CONV_K - 1:CONV_K, :] * y
    for j in range(CONV_K - 1):
        lo = CONV_HALO - (CONV_K - 1) + j
        acc = acc + cw[j:j + 1, :] * ext[lo:lo + rows, :]
    return acc


_ACTIVATIONS = {"none": lambda y: y, "silu": _silu, "sigmoid": _sigmoid}


def _inproj_kernel(h_ref, w_ref, cw_ref, p_ref, wb_ref, tail_ref, *, tile_modes, row_tiles_per_seq):
    j, i = pl.program_id(0), pl.program_id(1)

    @pl.when(i == 0)
    def _():
        for c0 in range(0, COL_TILE, 128):
            wb_ref[:, c0:c0 + 128] = w_ref[c0:c0 + 128, :].T.astype(BF16)

    def matmul():
        return jnp.dot(h_ref[...], wb_ref[...], preferred_element_type=F32)

    for lo, hi, mode in tile_modes:
        @pl.when((j >= lo) & (j < hi))
        def _(mode=mode):
            if mode != "conv_silu":
                p_ref[...] = _ACTIVATIONS[mode](matmul()).astype(p_ref.dtype)
                return

            @pl.when(i % row_tiles_per_seq == 0)
            def _():
                tail_ref[...] = jnp.zeros_like(tail_ref)

            cw = cw_ref[...]
            tail = tail_ref[...]
            y = matmul()
            p_ref[...] = _silu(_conv_taps(tail, y, cw)).astype(p_ref.dtype)
            tail_ref[...] = y[y.shape[0] - CONV_HALO:, :]


def _inproj(h2d, w_in_t, layer, conv_w, tile_modes, seq_len):
    m, d = h2d.shape
    n_main = tile_modes[-1][1] * COL_TILE
    n_conv = conv_w.shape[1] // COL_TILE
    body = functools.partial(_inproj_kernel, tile_modes=tile_modes, row_tiles_per_seq=seq_len // INPROJ_ROW_TILE)
    return pl.pallas_call(
        body,
        grid=(n_main // COL_TILE, m // INPROJ_ROW_TILE),
        in_specs=[
            pl.BlockSpec((INPROJ_ROW_TILE, d), lambda j, i: (i, 0)),
            pl.BlockSpec((None, COL_TILE, d), lambda j, i: (layer, j, 0)),
            pl.BlockSpec((CONV_K, COL_TILE), lambda j, i: (0, jnp.minimum(j, n_conv - 1))),
        ],
        out_specs=pl.BlockSpec((INPROJ_ROW_TILE, COL_TILE), lambda j, i: (i, j)),
        out_shape=jax.ShapeDtypeStruct((m, n_main), BF16),
        scratch_shapes=[pltpu.VMEM((d, COL_TILE), BF16),
                        pltpu.VMEM((CONV_HALO, COL_TILE), F32)],
        compiler_params=pltpu.CompilerParams(
            dimension_semantics=("arbitrary", "arbitrary"), vmem_limit_bytes=VMEM_LIMIT_BYTES),
        name="inproj",
    )(h2d, w_in_t, conv_w)


def _outproj_kernel(g_ref, w_ref, x_ref, nw_ref, wg_ref, xo_ref, h_ref, pg_ref, wb_ref):
    @pl.when(pl.program_id(0) == 0)
    def _():
        wb_ref[...] = w_ref[...].astype(BF16)

    y = x_ref[...] + jnp.dot(g_ref[...], wb_ref[...], preferred_element_type=F32)
    xo_ref[...] = y
    h = _rms(y, nw_ref[...]).astype(BF16)
    h_ref[...] = h
    pg_ref[...] = _gates_t(wg_ref, h)


def _outproj_final_kernel(g_ref, w_ref, x_ref, fw_ref, o_ref, wb_ref):
    @pl.when(pl.program_id(0) == 0)
    def _():
        wb_ref[...] = w_ref[...].astype(BF16)

    y = x_ref[...] + jnp.dot(g_ref[...], wb_ref[...], preferred_element_type=F32)
    o_ref[...] = _rms(y, fw_ref[...])


def _outproj(g2d, w_out, layer, x2d, nw_next, w_gate_next):
    m, d = x2d.shape
    k = g2d.shape[1]
    row = lambda i: (i, 0)
    fixed = lambda i: (0, 0)
    return pl.pallas_call(
        _outproj_kernel,
        grid=(m // OUTPROJ_ROW_TILE,),
        in_specs=[
            pl.BlockSpec((OUTPROJ_ROW_TILE, k), row),
            pl.BlockSpec((None, k, d), lambda i: (layer, 0, 0)),
            pl.BlockSpec((OUTPROJ_ROW_TILE, d), row),
            pl.BlockSpec((1, d), fixed),
            pl.BlockSpec((GATE_PAD, d), fixed),
        ],
        out_specs=[
            pl.BlockSpec((OUTPROJ_ROW_TILE, d), row),
            pl.BlockSpec((OUTPROJ_ROW_TILE, d), row),
            pl.BlockSpec((GATE_PAD, OUTPROJ_ROW_TILE), lambda i: (0, i)),
        ],
        out_shape=[jax.ShapeDtypeStruct((m, d), F32), jax.ShapeDtypeStruct((m, d), BF16),
                   jax.ShapeDtypeStruct((GATE_PAD, m), F32)],
        scratch_shapes=[pltpu.VMEM((k, d), BF16)],
        compiler_params=pltpu.CompilerParams(
            dimension_semantics=("arbitrary",), vmem_limit_bytes=OUTPROJ_VMEM_LIMIT_BYTES),
        name="outproj",
    )(g2d, w_out, x2d, nw_next.reshape(1, d), w_gate_next)


def _outproj_final(g2d, w_out, layer, x2d, final_w):
    m, d = x2d.shape
    k = g2d.shape[1]
    row = lambda i: (i, 0)
    return pl.pallas_call(
        _outproj_final_kernel,
        grid=(m // OUTPROJ_ROW_TILE,),
        in_specs=[
            pl.BlockSpec((OUTPROJ_ROW_TILE, k), row),
            pl.BlockSpec((None, k, d), lambda i: (layer, 0, 0)),
            pl.BlockSpec((OUTPROJ_ROW_TILE, d), row),
            pl.BlockSpec((1, d), lambda i: (0, 0)),
        ],
        out_specs=pl.BlockSpec((OUTPROJ_ROW_TILE, d), row),
        out_shape=jax.ShapeDtypeStruct((m, d), F32),
        scratch_shapes=[pltpu.VMEM((k, d), BF16)],
        compiler_params=pltpu.CompilerParams(
            dimension_semantics=("arbitrary",), vmem_limit_bytes=OUTPROJ_VMEM_LIMIT_BYTES),
        name="outproj_final",
    )(g2d, w_out, x2d, final_w.reshape(1, d))


def _chunk_iotas():
    r = lax.broadcasted_iota(jnp.int32, (CHUNK, CHUNK), 0)
    c = lax.broadcasted_iota(jnp.int32, (CHUNK, CHUNK), 1)
    return r, c


def _gate_spec(heads, last_block, kind, ahead):
    return pl.BlockSpec((None, heads, None, CHUNKS_PER_BLOCK, CHUNK),
                        lambda bi, h, ti: (kind, h, bi, jnp.minimum(ti + ahead, last_block), 0))


def _col(xt, n, width):
    return jnp.broadcast_to(xt[:, n:n + 1], (CHUNK, width))


def _columns(rows_list):
    rows = jnp.concatenate(rows_list, axis=0)
    if rows.shape[0] < 128:
        rows = jnp.concatenate([rows, jnp.zeros((128 - rows.shape[0], CHUNK), F32)], axis=0)
    rows = jnp.concatenate([rows, jnp.zeros((128, 128 - CHUNK), F32)], axis=1)
    return rows.T[:CHUNK, :]


def _unit_lower_inverse_minus_identity(a_list, r, c):
    base = 8
    blk = (r // base) == (c // base)
    a0 = [jnp.where(blk, a, 0.0) for a in a_list]
    a2 = [_dot(x, x) for x in a0]
    a34 = [_dot(jnp.concatenate([x, y], axis=0), y) for x, y in zip(a0, a2)]
    p = [y - x - z[:CHUNK] for x, y, z in zip(a0, a2, a34)]
    a4 = [z[CHUNK:] for z in a34]
    n = [x + y + _dot(x, y) for x, y in zip(p, a4)]
    s = base
    while s < CHUNK:
        off_mask = ((r // (2 * s)) == (c // (2 * s))) & (((r // s) % 2) == 1) & (((c // s) % 2) == 0)
        off = [jnp.where(off_mask, a, 0.0) for a in a_list]
        x = [o + _dot(o, m) for o, m in zip(off, n)]
        n = [m - y - _dot(m, y) for m, y in zip(n, x)]
        s *= 2
    return n


GDN_HEADS_PER_STEP = 4
_GDN_COLUMN_KINDS = 5


def _gdn_kernel(alog_ref, dtb_ref, q_ref, k_ref, v_ref, z_ref, a_ref, bt_ref, nw_ref, o_ref, s_ref):
    hp, nc = GDN_HEADS_PER_STEP, CHUNKS_PER_BLOCK
    hg = pl.program_id(1)

    @pl.when(pl.program_id(2) == 0)
    def _():
        s_ref[...] = jnp.zeros_like(s_ref)

    q = q_ref[0].astype(F32)
    k = k_ref[0].astype(F32)
    v = v_ref[0].astype(F32)

    r, c = _chunk_iotas()
    incl = r >= c
    strict = r > c
    upper = (r <= c).astype(F32)

    qh, kh, vh, gcs, es, rows = [], [], [], [], [], []
    for hd in range(hp):
        h = hg * hp + hd
        qs = q[:, hd * HEAD_QK:(hd + 1) * HEAD_QK]
        ks = k[:, hd * HEAD_QK:(hd + 1) * HEAD_QK]
        qh.append(qs * lax.rsqrt(jnp.sum(qs * qs, axis=-1, keepdims=True) + 1e-6) * (HEAD_QK ** -0.5))
        kh.append(ks * lax.rsqrt(jnp.sum(ks * ks, axis=-1, keepdims=True) + 1e-6))
        vh.append(v[:, hd * HEAD_V:(hd + 1) * HEAD_V])
        neg_rate = -jnp.exp(jnp.full((1, 1), alog_ref[h], F32))
        g = neg_rate * _softplus(a_ref[hd] + dtb_ref[h])
        beta = _sigmoid(bt_ref[hd])
        gc = _dot_f32(g, upper)
        e = jnp.exp(gc)
        gcs.append(gc)
        es.append(e)
        rows += [gc, beta, beta * e, e, jnp.exp(gc[:, CHUNK - 1:CHUNK] - gc)]
    heads_per_transpose = 128 // (_GDN_COLUMN_KINDS * nc)
    cols = [_columns(rows[i * _GDN_COLUMN_KINDS:(i + heads_per_transpose) * _GDN_COLUMN_KINDS])
            for i in range(0, hp, heads_per_transpose)]

    def col(hd, kind, n, width):
        i = ((hd % heads_per_transpose) * _GDN_COLUMN_KINDS + kind) * nc + n
        return _col(cols[hd // heads_per_transpose], i, width)

    chains = [(hd, n) for n in range(nc) for hd in range(hp)]
    q_c = {ch: qh[ch[0]][ch[1] * CHUNK:(ch[1] + 1) * CHUNK] for ch in chains}
    k_c = {ch: kh[ch[0]][ch[1] * CHUNK:(ch[1] + 1) * CHUNK] for ch in chains}
    v_c = {ch: vh[ch[0]][ch[1] * CHUNK:(ch[1] + 1) * CHUNK] for ch in chains}

    kq = {ch: _dot_nt(jnp.concatenate([k_c[ch], q_c[ch]], axis=0), k_c[ch]) for ch in chains}
    a_list, attn = [], {}
    for ch in chains:
        hd, n = ch
        gc_row = gcs[hd][n:n + 1, :]
        decay = jnp.where(incl, jnp.exp(jnp.where(incl, col(hd, 0, n, CHUNK) - gc_row, 0.0)), 0.0)
        a_list.append(jnp.where(strict, col(hd, 1, n, CHUNK) * kq[ch][:CHUNK] * decay, 0.0))
        attn[ch] = (kq[ch][CHUNK:] * decay).astype(BF16)
    n_list = _unit_lower_inverse_minus_identity(a_list, r, c)
    uw, kd = {}, {}
    for ch, nmat in zip(chains, n_list):
        hd, n = ch
        rhs = jnp.concatenate([col(hd, 1, n, HEAD_V) * v_c[ch], col(hd, 2, n, HEAD_QK) * k_c[ch]], axis=1)
        uw[ch] = (rhs + _dot(nmat, rhs)).astype(BF16)
        kd[ch] = (k_c[ch] * col(hd, 4, n, HEAD_QK)).astype(BF16)
    kd_uw = {ch: _dot_tn(kd[ch], uw[ch]) for ch in chains}
    at_uw = {ch: _dot(attn[ch], uw[ch]) for ch in chains}
    lhs = {}
    for ch in chains:
        hd, n = ch
        to_out = q_c[ch] * col(hd, 3, n, HEAD_QK) - at_uw[ch][:, HEAD_V:]
        lhs[ch] = jnp.concatenate([-kd_uw[ch][:, HEAD_V:], to_out], axis=0).astype(BF16)

    nw = nw_ref[...]
    s = [s_ref[hd] for hd in range(hp)]
    for n in range(nc):
        lo = n * CHUNK
        m = [_dot(lhs[(hd, n)], s[hd]) for hd in range(hp)]
        o = [m[hd][HEAD_QK:] + at_uw[(hd, n)][:, :HEAD_V] for hd in range(hp)]
        s = [es[hd][n:n + 1, CHUNK - 1:CHUNK] * s[hd] + m[hd][:HEAD_QK] + kd_uw[(hd, n)][:, :HEAD_V]
             for hd in range(hp)]
        for hd in range(hp):
            z = z_ref[0, lo:lo + CHUNK, hd * HEAD_V:(hd + 1) * HEAD_V]
            o_ref[0, lo:lo + CHUNK, hd * HEAD_V:(hd + 1) * HEAD_V] = (_rms(o[hd], nw) * z.astype(F32)).astype(o_ref.dtype)
    for hd in range(hp):
        s_ref[hd] = s[hd]


def _gdn_recurrence(p3, gates, a_log, dt_bias, norm_w):
    b, t, _ = p3.shape
    tb, hp = TIME_BLOCK, GDN_HEADS_PER_STEP
    wq, wv = hp * HEAD_QK, hp * HEAD_V
    nk = QK_W // wq
    nv = 2 * QK_W // wv
    nz = (2 * QK_W + D_INNER) // wv
    smem = pl.BlockSpec(memory_space=pltpu.SMEM)
    gate_spec = functools.partial(_gate_spec, hp, t // tb - 1)
    return pl.pallas_call(
        _gdn_kernel,
        grid=(b, N_HEADS // hp, t // tb),
        in_specs=[
            smem, smem,
            pl.BlockSpec((1, tb, wq), lambda bi, h, ti: (bi, ti, h)),
            pl.BlockSpec((1, tb, wq), lambda bi, h, ti: (bi, ti, nk + h)),
            pl.BlockSpec((1, tb, wv), lambda bi, h, ti: (bi, ti, nv + h)),
            pl.BlockSpec((1, tb, wv), lambda bi, h, ti: (bi, ti, nz + h)),
            gate_spec(0, 0), gate_spec(1, 0),
            pl.BlockSpec((1, HEAD_V), lambda bi, h, ti: (0, 0)),
        ],
        out_specs=pl.BlockSpec((1, tb, wv), lambda bi, h, ti: (bi, ti, h)),
        out_shape=jax.ShapeDtypeStruct((b, t, D_INNER), BF16),
        scratch_shapes=[pltpu.VMEM((hp, HEAD_QK, HEAD_V), F32)],
        compiler_params=pltpu.CompilerParams(
            dimension_semantics=("parallel", "parallel", "arbitrary"),
            vmem_limit_bytes=VMEM_LIMIT_BYTES),
        name="gdn_recurrence",
    )(a_log, dt_bias, p3, p3, p3, p3, gates, gates, norm_w.reshape(1, HEAD_V))


MLSTM_HEADS_PER_STEP = 2
_MLSTM_COLUMN_KINDS = 4


def _prefix_max_lanes(x):
    y = jnp.concatenate([x, jnp.full((x.shape[0], 128 - CHUNK), -jnp.inf, F32)], axis=1)
    s = 1
    while s < CHUNK:
        y = jnp.maximum(y, pltpu.roll(y, s, axis=1))
        s *= 2
    return y[:, :CHUNK]


MLSTM_CHUNK_GROUPS = 4


def _mlstm_gate_setup(ig_ref, fg_ref, ib_ref, fb_ref, hg, m_ref, cols_ref, d_ref, ws_ref):
    hp, nc = MLSTM_HEADS_PER_STEP, CHUNKS_PER_BLOCK
    r, c = _chunk_iotas()
    upper = (r <= c).astype(F32)
    chunk_id = lax.broadcasted_iota(jnp.int32, (nc, 1), 0)
    rows = []
    for hd in range(hp):
        h = hg * hp + hd
        i_pre = ig_ref[hd] + ib_ref[h]
        lf = -_softplus(-(fg_ref[hd] + fb_ref[h]))
        bc = _dot_f32(lf, upper)
        b_last = bc[:, CHUNK - 1:CHUNK]
        d = i_pre - bc
        g_end = b_last + d
        g_max = jnp.max(g_end, axis=-1, keepdims=True)
        r_max = bc + _prefix_max_lanes(d)
        m = m_ref[hd, 0:1, 0:1]
        m_before = jnp.zeros((nc, 1), F32)
        m_after = jnp.zeros((nc, 1), F32)
        for n in range(nc):
            m_before = jnp.where(chunk_id == n, m, m_before)
            m = jnp.maximum(b_last[n:n + 1, :] + m, g_max[n:n + 1, :])
            m_after = jnp.where(chunk_id == n, m, m_after)
        m_ref[hd] = jnp.broadcast_to(m, m_ref.shape[1:])
        m_t = jnp.maximum(bc + m_before, r_max)
        d_ref[hd] = d
        ws_ref[hd] = jnp.broadcast_to(jnp.exp(b_last + m_before - m_after), ws_ref.shape[1:])
        rows += [bc - m_t, jnp.exp(bc + m_before - m_t), jnp.exp(-m_t), jnp.exp(g_end - m_after)]
    cols_ref[...] = _columns(rows)


def _mlstm_kernel(ib_ref, fb_ref, q_ref, k_ref, v_ref, og_ref, z_ref, ig_ref, fg_ref, ig_next_ref, fg_next_ref,
                  nw_ref, o_ref, s_ref, m_ref, cols_ref, d_ref, ws_ref):
    hp, nc = MLSTM_HEADS_PER_STEP, CHUNKS_PER_BLOCK
    hg = pl.program_id(1)
    setup = functools.partial(_mlstm_gate_setup, ib_ref=ib_ref, fb_ref=fb_ref, hg=hg, m_ref=m_ref,
                              cols_ref=cols_ref, d_ref=d_ref, ws_ref=ws_ref)

    @pl.when(pl.program_id(2) == 0)
    def _():
        s_ref[...] = jnp.zeros_like(s_ref)
        m_ref[...] = jnp.zeros_like(m_ref)
        setup(ig_ref, fg_ref)

    cols = cols_ref[...]
    d_rows = [d_ref[hd] for hd in range(hp)]
    w_state = [ws_ref[hd][:, 0:1] for hd in range(hp)]
    setup(ig_next_ref, fg_next_ref)

    q = q_ref[0].astype(F32)
    k = k_ref[0].astype(F32) * (HEAD_QK ** -0.5)

    r, c = _chunk_iotas()
    incl = r >= c
    ones_col = (lax.broadcasted_iota(jnp.int32, (CHUNK, 128), 1) == 0).astype(BF16)

    def col(hd, kind, n, width):
        return _col(cols, (hd * _MLSTM_COLUMN_KINDS + kind) * nc + n, width)

    nw = nw_ref[...]
    s = [s_ref[hd] for hd in range(hp)]
    for grp in range(MLSTM_CHUNK_GROUPS):
        chunks = range(grp * nc // MLSTM_CHUNK_GROUPS, (grp + 1) * nc // MLSTM_CHUNK_GROUPS)
        chains = [(hd, n) for n in chunks for hd in range(hp)]
        q_c = {ch: q[ch[1] * CHUNK:(ch[1] + 1) * CHUNK, ch[0] * HEAD_QK:(ch[0] + 1) * HEAD_QK] for ch in chains}
        k_c = {ch: k[ch[1] * CHUNK:(ch[1] + 1) * CHUNK, ch[0] * HEAD_QK:(ch[0] + 1) * HEAD_QK] for ch in chains}
        v_ext = {ch: jnp.concatenate([v_ref[0, ch[1] * CHUNK:(ch[1] + 1) * CHUNK, ch[0] * HEAD_V:(ch[0] + 1) * HEAD_V],
                                      ones_col], axis=1) for ch in chains}

        qk = {ch: _dot_nt(q_c[ch], k_c[ch]) for ch in chains}
        amat, kw = {}, {}
        for ch in chains:
            hd, n = ch
            arg = col(hd, 0, n, CHUNK) + d_rows[hd][n:n + 1, :]
            amat[ch] = (qk[ch] * jnp.exp(jnp.where(incl, arg, -jnp.inf))).astype(BF16)
            kw[ch] = (k_c[ch] * col(hd, 3, n, HEAD_QK)).astype(BF16)
        av = {ch: _dot(amat[ch], v_ext[ch]) for ch in chains}
        kv = {ch: _dot_tn(kw[ch], v_ext[ch]) for ch in chains}

        s_in = {}
        for hd in range(hp):
            for n in chunks:
                s_in[(hd, n)] = s[hd]
                s[hd] = w_state[hd][n:n + 1, :] * s[hd] + kv[(hd, n)]
        qs = {ch: _dot(q_c[ch], s_in[ch]) for ch in chains}

        for ch in chains:
            hd, n = ch
            lo = n * CHUNK
            num_ext = col(hd, 1, n, HEAD_V + 128) * qs[ch] + av[ch]
            denom = jnp.maximum(jnp.abs(num_ext[:, HEAD_V:HEAD_V + 1]), col(hd, 2, n, 1))
            hh = num_ext[:, :HEAD_V] / denom
            gate = (og_ref[0, lo:lo + CHUNK, hd * HEAD_V:(hd + 1) * HEAD_V].astype(F32)
                    * z_ref[0, lo:lo + CHUNK, hd * HEAD_V:(hd + 1) * HEAD_V].astype(F32))
            o_ref[0, lo:lo + CHUNK, hd * HEAD_V:(hd + 1) * HEAD_V] = (_rms(hh, nw) * gate).astype(o_ref.dtype)
    for hd in range(hp):
        s_ref[hd] = s[hd]


def _mlstm_recurrence(p3, gates, i_bias, f_bias, norm_w):
    b, t, _ = p3.shape
    tb, hp = TIME_BLOCK, MLSTM_HEADS_PER_STEP
    wq, wv = hp * HEAD_QK, hp * HEAD_V
    nk = QK_W // wq
    nv = 2 * QK_W // wv
    nh = D_INNER // wv
    smem = pl.BlockSpec(memory_space=pltpu.SMEM)
    gate_spec = functools.partial(_gate_spec, hp, t // tb - 1)
    return pl.pallas_call(
        _mlstm_kernel,
        grid=(b, N_HEADS // hp, t // tb),
        in_specs=[
            smem, smem,
            pl.BlockSpec((1, tb, wq), lambda bi, h, ti: (bi, ti, h)),
            pl.BlockSpec((1, tb, wq), lambda bi, h, ti: (bi, ti, nk + h)),
            pl.BlockSpec((1, tb, wv), lambda bi, h, ti: (bi, ti, nv + h)),
            pl.BlockSpec((1, tb, wv), lambda bi, h, ti: (bi, ti, nv + nh + h)),
            pl.BlockSpec((1, tb, wv), lambda bi, h, ti: (bi, ti, nv + 2 * nh + h)),
            gate_spec(0, 0), gate_spec(1, 0), gate_spec(0, 1), gate_spec(1, 1),
            pl.BlockSpec((1, HEAD_V), lambda bi, h, ti: (0, 0)),
        ],
        out_specs=pl.BlockSpec((1, tb, wv), lambda bi, h, ti: (bi, ti, h)),
        out_shape=jax.ShapeDtypeStruct((b, t, D_INNER), BF16),
        scratch_shapes=[
            pltpu.VMEM((hp, HEAD_QK, HEAD_V + 128), F32),
            pltpu.VMEM((hp, 8, 128), F32),
            pltpu.VMEM((CHUNK, 128), F32),
            pltpu.VMEM((hp, CHUNKS_PER_BLOCK, CHUNK), F32),
            pltpu.VMEM((hp, CHUNKS_PER_BLOCK, 128), F32),
        ],
        compiler_params=pltpu.CompilerParams(
            dimension_semantics=("parallel", "parallel", "arbitrary"),
            vmem_limit_bytes=VMEM_LIMIT_BYTES),
        name="mlstm_recurrence",
    )(i_bias, f_bias, p3, p3, p3, p3, p3, gates, gates, gates, gates, norm_w.reshape(1, HEAD_V))


def _tile_modes(*widths_and_modes):
    ranges, lo = [], 0
    for width, mode in widths_and_modes:
        ranges.append((lo, lo + width // COL_TILE, mode))
        lo += width // COL_TILE
    return tuple(ranges)


_GDN_TILE_MODES = _tile_modes((2 * QK_W + D_INNER, "conv_silu"), (D_INNER, "silu"))
_MLSTM_TILE_MODES = _tile_modes((2 * QK_W, "conv_silu"), (D_INNER, "none"), (D_INNER, "sigmoid"),
                                (D_INNER, "silu"))


def _gate_weight_t(w_in_t, layer):
    n_main = w_in_t.shape[1] - 2 * N_HEADS
    return jnp.pad(w_in_t[layer, n_main:, :], ((0, GATE_PAD - 2 * N_HEADS), (0, 0)))


def _gate_rows(pg_t, b, t):
    return pg_t[:2 * N_HEADS].reshape(2, N_HEADS, b, t // CHUNK, CHUNK)


def kernel(x, norm_w, final_norm_w, gdn_w_in, gdn_conv_w, gdn_a_log, gdn_dt_bias, gdn_norm_w, gdn_w_out,
           mlstm_w_in, mlstm_conv_w, mlstm_i_bias, mlstm_f_bias, mlstm_norm_w, mlstm_w_out):
    b, t, d = x.shape
    depth = norm_w.shape[0]
    x2d = x.reshape(b * t, d)
    w_in_t = (jnp.swapaxes(gdn_w_in, 1, 2), jnp.swapaxes(mlstm_w_in, 1, 2))

    h, pg = _prenorm(x2d, norm_w[0], _gate_weight_t(w_in_t[0], 0))
    for i in range(depth):
        j = i // 2
        gates = _gate_rows(pg, b, t)
        if i % 2 == 0:
            p = _inproj(h, w_in_t[0], j, gdn_conv_w[j], _GDN_TILE_MODES, t).reshape(b, t, -1)
            g = _gdn_recurrence(p, gates, gdn_a_log[j], gdn_dt_bias[j], gdn_norm_w[j])
            w_out = gdn_w_out
        else:
            p = _inproj(h, w_in_t[1], j, mlstm_conv_w[j], _MLSTM_TILE_MODES, t).reshape(b, t, -1)
            g = _mlstm_recurrence(p, gates, mlstm_i_bias[j], mlstm_f_bias[j], mlstm_norm_w[j])
            w_out = mlstm_w_out
        g = g.reshape(b * t, D_INNER)
        if i == depth - 1:
            x2d = _outproj_final(g, w_out, j, x2d, final_norm_w)
        else:
            x2d, h, pg = _outproj(g, w_out, j, x2d, norm_w[i + 1], _gate_weight_t(w_in_t[(i + 1) % 2], (i + 1) // 2))
    return x2d.reshape(b, t, d)
```

```python
import functools

import jax
import jax.numpy as jnp
from jax import lax
from jax.experimental import pallas as pl
from jax.experimental.pallas import tpu as pltpu

F32 = jnp.float32
BF16 = jnp.bfloat16
HIGHEST = lax.Precision.HIGHEST

D_MODEL = 1024
N_HEADS = 8
HEAD_QK = 128
HEAD_V = 256
D_INNER = N_HEADS * HEAD_V
QK_W = N_HEADS * HEAD_QK
CONV_K = 4
CHUNK = 64
NORM_EPS = 1e-6
GATE_PAD = 128

INPROJ_ROW_TILE = 1024
ROW_TILE = 512
OUTPROJ_ROW_TILE = 1024
OUTPROJ_VMEM_LIMIT_BYTES = 56 * 1024 * 1024
COL_TILE = 1024
TIME_BLOCK = 512
CHUNKS_PER_BLOCK = TIME_BLOCK // CHUNK
CONV_HALO = 8
VMEM_LIMIT_BYTES = 48 * 1024 * 1024


def _dot(a, b):
    return jnp.dot(a.astype(BF16), b.astype(BF16), preferred_element_type=F32)


def _dot_nt(a, b):
    return lax.dot_general(a.astype(BF16), b.astype(BF16), (((1,), (1,)), ((), ())),
                           preferred_element_type=F32)


def _dot_tn(a, b):
    return lax.dot_general(a.astype(BF16), b.astype(BF16), (((0,), (0,)), ((), ())),
                           preferred_element_type=F32)


def _dot_f32(a, b):
    return jnp.dot(a, b, precision=HIGHEST, preferred_element_type=F32)


def _sigmoid(x):
    return jax.nn.sigmoid(x)


def _silu(x):
    return x * _sigmoid(x)


def _softplus(x):
    return jnp.maximum(x, 0.0) + jnp.log1p(jnp.exp(-jnp.abs(x)))


def _rms(x, w):
    return x * lax.rsqrt(jnp.mean(x * x, axis=-1, keepdims=True) + NORM_EPS) * w


def _gates_t(wg_ref, h):
    return lax.dot_general(wg_ref[...].astype(BF16), h, (((1,), (1,)), ((), ())), preferred_element_type=F32)


def _prenorm_kernel(x_ref, nw_ref, wg_ref, h_ref, pg_ref):
    h = _rms(x_ref[...], nw_ref[...]).astype(BF16)
    h_ref[...] = h
    pg_ref[...] = _gates_t(wg_ref, h)


def _prenorm(x2d, nw, w_gate):
    m, d = x2d.shape
    return pl.pallas_call(
        _prenorm_kernel,
        grid=(m // ROW_TILE,),
        in_specs=[
            pl.BlockSpec((ROW_TILE, d), lambda i: (i, 0)),
            pl.BlockSpec((1, d), lambda i: (0, 0)),
            pl.BlockSpec((GATE_PAD, d), lambda i: (0, 0)),
        ],
        out_specs=[
            pl.BlockSpec((ROW_TILE, d), lambda i: (i, 0)),
            pl.BlockSpec((GATE_PAD, ROW_TILE), lambda i: (0, i)),
        ],
        out_shape=[jax.ShapeDtypeStruct((m, d), BF16), jax.ShapeDtypeStruct((GATE_PAD, m), F32)],
        compiler_params=pltpu.CompilerParams(
            dimension_semantics=("parallel",), vmem_limit_bytes=VMEM_LIMIT_BYTES),
        name="prenorm",
    )(x2d, nw.reshape(1, d), w_gate)


def _conv_taps(tail, y, cw):
    rows = y.shape[0]
    ext = jnp.concatenate([tail, y], axis=0)
    acc = cw[CONV_K - 1:CONV_K, :] * y
    for j in range(CONV_K - 1):
        lo = CONV_HALO - (CONV_K - 1) + j
        acc = acc + cw[j:j + 1, :] * ext[lo:lo + rows, :]
    return acc


_ACTIVATIONS = {"none": lambda y: y, "silu": _silu, "sigmoid": _sigmoid}


def _inproj_kernel(h_ref, w_ref, cw_ref, p_ref, wb_ref, tail_ref, *, tile_modes, row_tiles_per_seq):
    j, i = pl.program_id(0), pl.program_id(1)

    @pl.when(i == 0)
    def _():
        for c0 in range(0, COL_TILE, 128):
            wb_ref[:, c0:c0 + 128] = w_ref[c0:c0 + 128, :].T.astype(BF16)

    def matmul():
        return jnp.dot(h_ref[...], wb_ref[...], preferred_element_type=F32)

    for lo, hi, mode in tile_modes:
        @pl.when((j >= lo) & (j < hi))
        def _(mode=mode):
            if mode != "conv_silu":
                p_ref[...] = _ACTIVATIONS[mode](matmul()).astype(p_ref.dtype)
                return

            @pl.when(i % row_tiles_per_seq == 0)
            def _():
                tail_ref[...] = jnp.zeros_like(tail_ref)

            cw = cw_ref[...]
            tail = tail_ref[...]
            y = matmul()
            p_ref[...] = _silu(_conv_taps(tail, y, cw)).astype(p_ref.dtype)
            tail_ref[...] = y[y.shape[0] - CONV_HALO:, :]


def _inproj(h2d, w_in_t, layer, conv_w, tile_modes, seq_len):
    m, d = h2d.shape
    n_main = tile_modes[-1][1] * COL_TILE
    n_conv = conv_w.shape[1] // COL_TILE
    body = functools.partial(_inproj_kernel, tile_modes=tile_modes, row_tiles_per_seq=seq_len // INPROJ_ROW_TILE)
    return pl.pallas_call(
        body,
        grid=(n_main // COL_TILE, m // INPROJ_ROW_TILE),
        in_specs=[
            pl.BlockSpec((INPROJ_ROW_TILE, d), lambda j, i: (i, 0)),
            pl.BlockSpec((None, COL_TILE, d), lambda j, i: (layer, j, 0)),
            pl.BlockSpec((CONV_K, COL_TILE), lambda j, i: (0, jnp.minimum(j, n_conv - 1))),
        ],
        out_specs=pl.BlockSpec((INPROJ_ROW_TILE, COL_TILE), lambda j, i: (i, j)),
        out_shape=jax.ShapeDtypeStruct((m, n_main), BF16),
        scratch_shapes=[pltpu.VMEM((d, COL_TILE), BF16),
                        pltpu.VMEM((CONV_HALO, COL_TILE), F32)],
        compiler_params=pltpu.CompilerParams(
            dimension_semantics=("arbitrary", "arbitrary"), vmem_limit_bytes=VMEM_LIMIT_BYTES),
        name="inproj",
    )(h2d, w_in_t, conv_w)


def _outproj_kernel(g_ref, w_ref, x_ref, nw_ref, wg_ref, xo_ref, h_ref, pg_ref, wb_ref):
    @pl.when(pl.program_id(0) == 0)
    def _():
        wb_ref[...] = w_ref[...].astype(BF16)

    y = x_ref[...] + jnp.dot(g_ref[...], wb_ref[...], preferred_element_type=F32)
    xo_ref[...] = y
    h = _rms(y, nw_ref[...]).astype(BF16)
    h_ref[...] = h
    pg_ref[...] = _gates_t(wg_ref, h)


def _outproj_final_kernel(g_ref, w_ref, x_ref, fw_ref, o_ref, wb_ref):
    @pl.when(pl.program_id(0) == 0)
    def _():
        wb_ref[...] = w_ref[...].astype(BF16)

    y = x_ref[...] + jnp.dot(g_ref[...], wb_ref[...], preferred_element_type=F32)
    o_ref[...] = _rms(y, fw_ref[...])


def _outproj(g2d, w_out, layer, x2d, nw_next, w_gate_next):
    m, d = x2d.shape
    k = g2d.shape[1]
    row = lambda i: (i, 0)
    fixed = lambda i: (0, 0)
    return pl.pallas_call(
        _outproj_kernel,
        grid=(m // OUTPROJ_ROW_TILE,),
        in_specs=[
            pl.BlockSpec((OUTPROJ_ROW_TILE, k), row),
            pl.BlockSpec((None, k, d), lambda i: (layer, 0, 0)),
            pl.BlockSpec((OUTPROJ_ROW_TILE, d), row),
            pl.BlockSpec((1, d), fixed),
            pl.BlockSpec((GATE_PAD, d), fixed),
        ],
        out_specs=[
            pl.BlockSpec((OUTPROJ_ROW_TILE, d), row),
            pl.BlockSpec((OUTPROJ_ROW_TILE, d), row),
            pl.BlockSpec((GATE_PAD, OUTPROJ_ROW_TILE), lambda i: (0, i)),
        ],
        out_shape=[jax.ShapeDtypeStruct((m, d), F32), jax.ShapeDtypeStruct((m, d), BF16),
                   jax.ShapeDtypeStruct((GATE_PAD, m), F32)],
        scratch_shapes=[pltpu.VMEM((k, d), BF16)],
        compiler_params=pltpu.CompilerParams(
            dimension_semantics=("arbitrary",), vmem_limit_bytes=OUTPROJ_VMEM_LIMIT_BYTES),
        name="outproj",
    )(g2d, w_out, x2d, nw_next.reshape(1, d), w_gate_next)


def _outproj_final(g2d, w_out, layer, x2d, final_w):
    m, d = x2d.shape
    k = g2d.shape[1]
    row = lambda i: (i, 0)
    return pl.pallas_call(
        _outproj_final_kernel,
        grid=(m // OUTPROJ_ROW_TILE,),
        in_specs=[
            pl.BlockSpec((OUTPROJ_ROW_TILE, k), row),
            pl.BlockSpec((None, k, d), lambda i: (layer, 0, 0)),
            pl.BlockSpec((OUTPROJ_ROW_TILE, d), row),
            pl.BlockSpec((1, d), lambda i: (0, 0)),
        ],
        out_specs=pl.BlockSpec((OUTPROJ_ROW_TILE, d), row),
        out_shape=jax.ShapeDtypeStruct((m, d), F32),
        scratch_shapes=[pltpu.VMEM((k, d), BF16)],
        compiler_params=pltpu.CompilerParams(
            dimension_semantics=("arbitrary",), vmem_limit_bytes=OUTPROJ_VMEM_LIMIT_BYTES),
        name="outproj_final",
    )(g2d, w_out, x2d, final_w.reshape(1, d))


def _chunk_iotas():
    r = lax.broadcasted_iota(jnp.int32, (CHUNK, CHUNK), 0)
    c = lax.broadcasted_iota(jnp.int32, (CHUNK, CHUNK), 1)
    return r, c


def _gate_spec(heads, last_block, kind, ahead):
    return pl.BlockSpec((None, heads, None, CHUNKS_PER_BLOCK, CHUNK),
                        lambda bi, h, ti: (kind, h, bi, jnp.minimum(ti + ahead, last_block), 0))


def _col(xt, n, width):
    return jnp.broadcast_to(xt[:, n:n + 1], (CHUNK, width))


def _columns(rows_list):
    rows = jnp.concatenate(rows_list, axis=0)
    if rows.shape[0] < 128:
        rows = jnp.concatenate([rows, jnp.zeros((128 - rows.shape[0], CHUNK), F32)], axis=0)
    rows = jnp.concatenate([rows, jnp.zeros((128, 128 - CHUNK), F32)], axis=1)
    return rows.T[:CHUNK, :]


def _unit_lower_inverse_minus_identity(a_list, r, c):
    base = 8
    blk = (r // base) == (c // base)
    a0 = [jnp.where(blk, a, 0.0) for a in a_list]
    a2 = [_dot(x, x) for x in a0]
    a34 = [_dot(jnp.concatenate([x, y], axis=0), y) for x, y in zip(a0, a2)]
    p = [y - x - z[:CHUNK] for x, y, z in zip(a0, a2, a34)]
    a4 = [z[CHUNK:] for z in a34]
    n = [x + y + _dot(x, y) for x, y in zip(p, a4)]
    s = base
    while s < CHUNK:
        off_mask = ((r // (2 * s)) == (c // (2 * s))) & (((r // s) % 2) == 1) & (((c // s) % 2) == 0)
        off = [jnp.where(off_mask, a, 0.0) for a in a_list]
        x = [o + _dot(o, m) for o, m in zip(off, n)]
        n = [m - y - _dot(m, y) for m, y in zip(n, x)]
        s *= 2
    return n


GDN_HEADS_PER_STEP = 8
_GDN_COLUMN_KINDS = 5


def _gdn_kernel(alog_ref, dtb_ref, q_ref, k_ref, v_ref, z_ref, a_ref, bt_ref, nw_ref, o_ref, s_ref):
    hp, nc = GDN_HEADS_PER_STEP, CHUNKS_PER_BLOCK
    hg = pl.program_id(1)

    @pl.when(pl.program_id(2) == 0)
    def _():
        s_ref[...] = jnp.zeros_like(s_ref)

    q = q_ref[0].astype(F32)
    k = k_ref[0].astype(F32)
    v = v_ref[0].astype(F32)

    r, c = _chunk_iotas()
    incl = r >= c
    strict = r > c
    upper = (r <= c).astype(F32)

    qh, kh, vh, gcs, es, rows = [], [], [], [], [], []
    for hd in range(hp):
        h = hg * hp + hd
        qs = q[:, hd * HEAD_QK:(hd + 1) * HEAD_QK]
        ks = k[:, hd * HEAD_QK:(hd + 1) * HEAD_QK]
        qh.append(qs * lax.rsqrt(jnp.sum(qs * qs, axis=-1, keepdims=True) + 1e-6) * (HEAD_QK ** -0.5))
        kh.append(ks * lax.rsqrt(jnp.sum(ks * ks, axis=-1, keepdims=True) + 1e-6))
        vh.append(v[:, hd * HEAD_V:(hd + 1) * HEAD_V])
        neg_rate = -jnp.exp(jnp.full((1, 1), alog_ref[h], F32))
        g = neg_rate * _softplus(a_ref[hd] + dtb_ref[h])
        beta = _sigmoid(bt_ref[hd])
        gc = _dot_f32(g, upper)
        e = jnp.exp(gc)
        gcs.append(gc)
        es.append(e)
        rows += [gc, beta, beta * e, e, jnp.exp(gc[:, CHUNK - 1:CHUNK] - gc)]
    heads_per_transpose = 128 // (_GDN_COLUMN_KINDS * nc)
    cols = [_columns(rows[i * _GDN_COLUMN_KINDS:(i + heads_per_transpose) * _GDN_COLUMN_KINDS])
            for i in range(0, hp, heads_per_transpose)]

    def col(hd, kind, n, width):
        i = ((hd % heads_per_transpose) * _GDN_COLUMN_KINDS + kind) * nc + n
        return _col(cols[hd // heads_per_transpose], i, width)

    chains = [(hd, n) for n in range(nc) for hd in range(hp)]
    q_c = {ch: qh[ch[0]][ch[1] * CHUNK:(ch[1] + 1) * CHUNK] for ch in chains}
    k_c = {ch: kh[ch[0]][ch[1] * CHUNK:(ch[1] + 1) * CHUNK] for ch in chains}
    v_c = {ch: vh[ch[0]][ch[1] * CHUNK:(ch[1] + 1) * CHUNK] for ch in chains}

    kq = {ch: _dot_nt(jnp.concatenate([k_c[ch], q_c[ch]], axis=0), k_c[ch]) for ch in chains}
    a_list, attn = [], {}
    for ch in chains:
        hd, n = ch
        gc_row = gcs[hd][n:n + 1, :]
        decay = jnp.where(incl, jnp.exp(jnp.where(incl, col(hd, 0, n, CHUNK) - gc_row, 0.0)), 0.0)
        a_list.append(jnp.where(strict, col(hd, 1, n, CHUNK) * kq[ch][:CHUNK] * decay, 0.0))
        attn[ch] = (kq[ch][CHUNK:] * decay).astype(BF16)
    n_list = _unit_lower_inverse_minus_identity(a_list, r, c)
    uw, kd = {}, {}
    for ch, nmat in zip(chains, n_list):
        hd, n = ch
        rhs = jnp.concatenate([col(hd, 1, n, HEAD_V) * v_c[ch], col(hd, 2, n, HEAD_QK) * k_c[ch]], axis=1)
        uw[ch] = (rhs + _dot(nmat, rhs)).astype(BF16)
        kd[ch] = (k_c[ch] * col(hd, 4, n, HEAD_QK)).astype(BF16)
    kd_uw = {ch: _dot_tn(kd[ch], uw[ch]) for ch in chains}
    at_uw = {ch: _dot(attn[ch], uw[ch]) for ch in chains}
    lhs = {}
    for ch in chains:
        hd, n = ch
        to_out = q_c[ch] * col(hd, 3, n, HEAD_QK) - at_uw[ch][:, HEAD_V:]
        lhs[ch] = jnp.concatenate([-kd_uw[ch][:, HEAD_V:], to_out], axis=0).astype(BF16)

    nw = nw_ref[...]
    s = [s_ref[hd] for hd in range(hp)]
    for n in range(nc):
        lo = n * CHUNK
        m = [_dot(lhs[(hd, n)], s[hd]) for hd in range(hp)]
        o = [m[hd][HEAD_QK:] + at_uw[(hd, n)][:, :HEAD_V] for hd in range(hp)]
        s = [es[hd][n:n + 1, CHUNK - 1:CHUNK] * s[hd] + m[hd][:HEAD_QK] + kd_uw[(hd, n)][:, :HEAD_V]
             for hd in range(hp)]
        for hd in range(hp):
            z = z_ref[0, lo:lo + CHUNK, hd * HEAD_V:(hd + 1) * HEAD_V]
            o_ref[0, lo:lo + CHUNK, hd * HEAD_V:(hd + 1) * HEAD_V] = (_rms(o[hd], nw) * z.astype(F32)).astype(o_ref.dtype)
    for hd in range(hp):
        s_ref[hd] = s[hd]


def _gdn_recurrence(p3, gates, a_log, dt_bias, norm_w):
    b, t, _ = p3.shape
    tb, hp = TIME_BLOCK, GDN_HEADS_PER_STEP
    wq, wv = hp * HEAD_QK, hp * HEAD_V
    nk = QK_W // wq
    nv = 2 * QK_W // wv
    nz = (2 * QK_W + D_INNER) // wv
    smem = pl.BlockSpec(memory_space=pltpu.SMEM)
    gate_spec = functools.partial(_gate_spec, hp, t // tb - 1)
    return pl.pallas_call(
        _gdn_kernel,
        grid=(b, N_HEADS // hp, t // tb),
        in_specs=[
            smem, smem,
            pl.BlockSpec((1, tb, wq), lambda bi, h, ti: (bi, ti, h)),
            pl.BlockSpec((1, tb, wq), lambda bi, h, ti: (bi, ti, nk + h)),
            pl.BlockSpec((1, tb, wv), lambda bi, h, ti: (bi, ti, nv + h)),
            pl.BlockSpec((1, tb, wv), lambda bi, h, ti: (bi, ti, nz + h)),
            gate_spec(0, 0), gate_spec(1, 0),
            pl.BlockSpec((1, HEAD_V), lambda bi, h, ti: (0, 0)),
        ],
        out_specs=pl.BlockSpec((1, tb, wv), lambda bi, h, ti: (bi, ti, h)),
        out_shape=jax.ShapeDtypeStruct((b, t, D_INNER), BF16),
        scratch_shapes=[pltpu.VMEM((hp, HEAD_QK, HEAD_V), F32)],
        compiler_params=pltpu.CompilerParams(
            dimension_semantics=("parallel", "parallel", "arbitrary"),
            vmem_limit_bytes=VMEM_LIMIT_BYTES),
        name="gdn_recurrence",
    )(a_log, dt_bias, p3, p3, p3, p3, gates, gates, norm_w.reshape(1, HEAD_V))


MLSTM_HEADS_PER_STEP = 2
_MLSTM_COLUMN_KINDS = 4


def _prefix_max_lanes(x):
    y = jnp.concatenate([x, jnp.full((x.shape[0], 128 - CHUNK), -jnp.inf, F32)], axis=1)
    s = 1
    while s < CHUNK:
        y = jnp.maximum(y, pltpu.roll(y, s, axis=1))
        s *= 2
    return y[:, :CHUNK]


MLSTM_CHUNK_GROUPS = 4


def _mlstm_gate_setup(ig_ref, fg_ref, ib_ref, fb_ref, hg, m_ref, cols_ref, d_ref, ws_ref):
    hp, nc = MLSTM_HEADS_PER_STEP, CHUNKS_PER_BLOCK
    r, c = _chunk_iotas()
    upper = (r <= c).astype(F32)
    chunk_id = lax.broadcasted_iota(jnp.int32, (nc, 1), 0)
    rows = []
    for hd in range(hp):
        h = hg * hp + hd
        i_pre = ig_ref[hd] + ib_ref[h]
        lf = -_softplus(-(fg_ref[hd] + fb_ref[h]))
        bc = _dot_f32(lf, upper)
        b_last = bc[:, CHUNK - 1:CHUNK]
        d = i_pre - bc
        g_end = b_last + d
        g_max = jnp.max(g_end, axis=-1, keepdims=True)
        r_max = bc + _prefix_max_lanes(d)
        m = m_ref[hd, 0:1, 0:1]
        m_before = jnp.zeros((nc, 1), F32)
        m_after = jnp.zeros((nc, 1), F32)
        for n in range(nc):
            m_before = jnp.where(chunk_id == n, m, m_before)
            m = jnp.maximum(b_last[n:n + 1, :] + m, g_max[n:n + 1, :])
            m_after = jnp.where(chunk_id == n, m, m_after)
        m_ref[hd] = jnp.broadcast_to(m, m_ref.shape[1:])
        m_t = jnp.maximum(bc + m_before, r_max)
        d_ref[hd] = d
        ws_ref[hd] = jnp.broadcast_to(jnp.exp(b_last + m_before - m_after), ws_ref.shape[1:])
        rows += [bc - m_t, jnp.exp(bc + m_before - m_t), jnp.exp(-m_t), jnp.exp(g_end - m_after)]
    cols_ref[...] = _columns(rows)


def _mlstm_kernel(ib_ref, fb_ref, q_ref, k_ref, v_ref, og_ref, z_ref, ig_ref, fg_ref, ig_next_ref, fg_next_ref,
                  nw_ref, o_ref, s_ref, m_ref, cols_ref, d_ref, ws_ref):
    hp, nc = MLSTM_HEADS_PER_STEP, CHUNKS_PER_BLOCK
    hg = pl.program_id(1)
    setup = functools.partial(_mlstm_gate_setup, ib_ref=ib_ref, fb_ref=fb_ref, hg=hg, m_ref=m_ref,
                              cols_ref=cols_ref, d_ref=d_ref, ws_ref=ws_ref)

    @pl.when(pl.program_id(2) == 0)
    def _():
        s_ref[...] = jnp.zeros_like(s_ref)
        m_ref[...] = jnp.zeros_like(m_ref)
        setup(ig_ref, fg_ref)

    cols = cols_ref[...]
    d_rows = [d_ref[hd] for hd in range(hp)]
    w_state = [ws_ref[hd][:, 0:1] for hd in range(hp)]
    setup(ig_next_ref, fg_next_ref)

    q = q_ref[0].astype(F32)
    k = k_ref[0].astype(F32) * (HEAD_QK ** -0.5)

    r, c = _chunk_iotas()
    incl = r >= c
    ones_col = (lax.broadcasted_iota(jnp.int32, (CHUNK, 128), 1) == 0).astype(BF16)

    def col(hd, kind, n, width):
        return _col(cols, (hd * _MLSTM_COLUMN_KINDS + kind) * nc + n, width)

    nw = nw_ref[...]
    s = [s_ref[hd] for hd in range(hp)]
    for grp in range(MLSTM_CHUNK_GROUPS):
        chunks = range(grp * nc // MLSTM_CHUNK_GROUPS, (grp + 1) * nc // MLSTM_CHUNK_GROUPS)
        chains = [(hd, n) for n in chunks for hd in range(hp)]
        q_c = {ch: q[ch[1] * CHUNK:(ch[1] + 1) * CHUNK, ch[0] * HEAD_QK:(ch[0] + 1) * HEAD_QK] for ch in chains}
        k_c = {ch: k[ch[1] * CHUNK:(ch[1] + 1) * CHUNK, ch[0] * HEAD_QK:(ch[0] + 1) * HEAD_QK] for ch in chains}
        v_ext = {ch: jnp.concatenate([v_ref[0, ch[1] * CHUNK:(ch[1] + 1) * CHUNK, ch[0] * HEAD_V:(ch[0] + 1) * HEAD_V],
                                      ones_col], axis=1) for ch in chains}

        qk = {ch: _dot_nt(q_c[ch], k_c[ch]) for ch in chains}
        amat, kw = {}, {}
        for ch in chains:
            hd, n = ch
            arg = col(hd, 0, n, CHUNK) + d_rows[hd][n:n + 1, :]
            amat[ch] = (qk[ch] * jnp.exp(jnp.where(incl, arg, -jnp.inf))).astype(BF16)
            kw[ch] = (k_c[ch] * col(hd, 3, n, HEAD_QK)).astype(BF16)
        av = {ch: _dot(amat[ch], v_ext[ch]) for ch in chains}
        kv = {ch: _dot_tn(kw[ch], v_ext[ch]) for ch in chains}

        s_in = {}
        for hd in range(hp):
            for n in chunks:
                s_in[(hd, n)] = s[hd]
                s[hd] = w_state[hd][n:n + 1, :] * s[hd] + kv[(hd, n)]
        qs = {ch: _dot(q_c[ch], s_in[ch]) for ch in chains}

        for ch in chains:
            hd, n = ch
            lo = n * CHUNK
            num_ext = col(hd, 1, n, HEAD_V + 128) * qs[ch] + av[ch]
            denom = jnp.maximum(jnp.abs(num_ext[:, HEAD_V:HEAD_V + 1]), col(hd, 2, n, 1))
            hh = num_ext[:, :HEAD_V] / denom
            gate = (og_ref[0, lo:lo + CHUNK, hd * HEAD_V:(hd + 1) * HEAD_V].astype(F32)
                    * z_ref[0, lo:lo + CHUNK, hd * HEAD_V:(hd + 1) * HEAD_V].astype(F32))
            o_ref[0, lo:lo + CHUNK, hd * HEAD_V:(hd + 1) * HEAD_V] = (_rms(hh, nw) * gate).astype(o_ref.dtype)
    for hd in range(hp):
        s_ref[hd] = s[hd]


def _mlstm_recurrence(p3, gates, i_bias, f_bias, norm_w):
    b, t, _ = p3.shape
    tb, hp = TIME_BLOCK, MLSTM_HEADS_PER_STEP
    wq, wv = hp * HEAD_QK, hp * HEAD_V
    nk = QK_W // wq
    nv = 2 * QK_W // wv
    nh = D_INNER // wv
    smem = pl.BlockSpec(memory_space=pltpu.SMEM)
    gate_spec = functools.partial(_gate_spec, hp, t // tb - 1)
    return pl.pallas_call(
        _mlstm_kernel,
        grid=(b, N_HEADS // hp, t // tb),
        in_specs=[
            smem, smem,
            pl.BlockSpec((1, tb, wq), lambda bi, h, ti: (bi, ti, h)),
            pl.BlockSpec((1, tb, wq), lambda bi, h, ti: (bi, ti, nk + h)),
            pl.BlockSpec((1, tb, wv), lambda bi, h, ti: (bi, ti, nv + h)),
            pl.BlockSpec((1, tb, wv), lambda bi, h, ti: (bi, ti, nv + nh + h)),
            pl.BlockSpec((1, tb, wv), lambda bi, h, ti: (bi, ti, nv + 2 * nh + h)),
            gate_spec(0, 0), gate_spec(1, 0), gate_spec(0, 1), gate_spec(1, 1),
            pl.BlockSpec((1, HEAD_V), lambda bi, h, ti: (0, 0)),
        ],
        out_specs=pl.BlockSpec((1, tb, wv), lambda bi, h, ti: (bi, ti, h)),
        out_shape=jax.ShapeDtypeStruct((b, t, D_INNER), BF16),
        scratch_shapes=[
            pltpu.VMEM((hp, HEAD_QK, HEAD_V + 128), F32),
            pltpu.VMEM((hp, 8, 128), F32),
            pltpu.VMEM((CHUNK, 128), F32),
            pltpu.VMEM((hp, CHUNKS_PER_BLOCK, CHUNK), F32),
            pltpu.VMEM((hp, CHUNKS_PER_BLOCK, 128), F32),
        ],
        compiler_params=pltpu.CompilerParams(
            dimension_semantics=("parallel", "parallel", "arbitrary"),
            vmem_limit_bytes=VMEM_LIMIT_BYTES),
        name="mlstm_recurrence",
    )(i_bias, f_bias, p3, p3, p3, p3, p3, gates, gates, gates, gates, norm_w.reshape(1, HEAD_V))


def _tile_modes(*widths_and_modes):
    ranges, lo = [], 0
    for width, mode in widths_and_modes:
        ranges.append((lo, lo + width // COL_TILE, mode))
        lo += width // COL_TILE
    return tuple(ranges)


_GDN_TILE_MODES = _tile_modes((2 * QK_W + D_INNER, "conv_silu"), (D_INNER, "silu"))
_MLSTM_TILE_MODES = _tile_modes((2 * QK_W, "conv_silu"), (D_INNER, "none"), (D_INNER, "sigmoid"),
                                (D_INNER, "silu"))


def _gate_weight_t(w_in_t, layer):
    n_main = w_in_t.shape[1] - 2 * N_HEADS
    return jnp.pad(w_in_t[layer, n_main:, :], ((0, GATE_PAD - 2 * N_HEADS), (0, 0)))


def _gate_rows(pg_t, b, t):
    return pg_t[:2 * N_HEADS].reshape(2, N_HEADS, b, t // CHUNK, CHUNK)


def kernel(x, norm_w, final_norm_w, gdn_w_in, gdn_conv_w, gdn_a_log, gdn_dt_bias, gdn_norm_w, gdn_w_out,
           mlstm_w_in, mlstm_conv_w, mlstm_i_bias, mlstm_f_bias, mlstm_norm_w, mlstm_w_out):
    b, t, d = x.shape
    depth = norm_w.shape[0]
    x2d = x.reshape(b * t, d)
    w_in_t = (jnp.swapaxes(gdn_w_in, 1, 2), jnp.swapaxes(mlstm_w_in, 1, 2))

    h, pg = _prenorm(x2d, norm_w[0], _gate_weight_t(w_in_t[0], 0))
    for i in range(depth):
        j = i // 2
        gates = _gate_rows(pg, b, t)
        if i % 2 == 0:
            p = _inproj(h, w_in_t[0], j, gdn_conv_w[j], _GDN_TILE_MODES, t).reshape(b, t, -1)
            g = _gdn_recurrence(p, gates, gdn_a_log[j], gdn_dt_bias[j], gdn_norm_w[j])
            w_out = gdn_w_out
        else:
            p = _inproj(h, w_in_t[1], j, mlstm_conv_w[j], _MLSTM_TILE_MODES, t).reshape(b, t, -1)
            g = _mlstm_recurrence(p, gates, mlstm_i_bias[j], mlstm_f_bias[j], mlstm_norm_w[j])
            w_out = mlstm_w_out
        g = g.reshape(b * t, D_INNER)
        if i == depth - 1:
            x2d = _outproj_final(g, w_out, j, x2d, final_norm_w)
        else:
            x2d, h, pg = _outproj(g, w_out, j, x2d, norm_w[i + 1], _gate_weight_t(w_in_t[(i + 1) % 2], (i + 1) // 2))
    return x2d.reshape(b, t, d)
```

```python
import functools

import jax
import jax.numpy as jnp
from jax import lax
from jax.experimental import pallas as pl
from jax.experimental.pallas import tpu as pltpu

F32 = jnp.float32
BF16 = jnp.bfloat16
HIGHEST = lax.Precision.HIGHEST

D_MODEL = 1024
N_HEADS = 8
HEAD_QK = 128
HEAD_V = 256
D_INNER = N_HEADS * HEAD_V
QK_W = N_HEADS * HEAD_QK
CONV_K = 4
CHUNK = 64
NORM_EPS = 1e-6
GATE_PAD = 128

INPROJ_ROW_TILE = 1024
ROW_TILE = 512
OUTPROJ_ROW_TILE = 1024
OUTPROJ_VMEM_LIMIT_BYTES = 56 * 1024 * 1024
COL_TILE = 2048
TIME_BLOCK = 512
CHUNKS_PER_BLOCK = TIME_BLOCK // CHUNK
CONV_HALO = 8
VMEM_LIMIT_BYTES = 48 * 1024 * 1024


def _dot(a, b):
    return jnp.dot(a.astype(BF16), b.astype(BF16), preferred_element_type=F32)


def _dot_nt(a, b):
    return lax.dot_general(a.astype(BF16), b.astype(BF16), (((1,), (1,)), ((), ())),
                           preferred_element_type=F32)


def _dot_tn(a, b):
    return lax.dot_general(a.astype(BF16), b.astype(BF16), (((0,), (0,)), ((), ())),
                           preferred_element_type=F32)


def _dot_f32(a, b):
    return jnp.dot(a, b, precision=HIGHEST, preferred_element_type=F32)


def _sigmoid(x):
    return jax.nn.sigmoid(x)


def _silu(x):
    return x * _sigmoid(x)


def _softplus(x):
    return jnp.maximum(x, 0.0) + jnp.log1p(jnp.exp(-jnp.abs(x)))


def _rms(x, w):
    return x * lax.rsqrt(jnp.mean(x * x, axis=-1, keepdims=True) + NORM_EPS) * w


def _gates_t(wg_ref, h):
    return lax.dot_general(wg_ref[...].astype(BF16), h, (((1,), (1,)), ((), ())), preferred_element_type=F32)


def _prenorm_kernel(x_ref, nw_ref, wg_ref, h_ref, pg_ref):
    h = _rms(x_ref[...], nw_ref[...]).astype(BF16)
    h_ref[...] = h
    pg_ref[...] = _gates_t(wg_ref, h)


def _prenorm(x2d, nw, w_gate):
    m, d = x2d.shape
    return pl.pallas_call(
        _prenorm_kernel,
        grid=(m // ROW_TILE,),
        in_specs=[
            pl.BlockSpec((ROW_TILE, d), lambda i: (i, 0)),
            pl.BlockSpec((1, d), lambda i: (0, 0)),
            pl.BlockSpec((GATE_PAD, d), lambda i: (0, 0)),
        ],
        out_specs=[
            pl.BlockSpec((ROW_TILE, d), lambda i: (i, 0)),
            pl.BlockSpec((GATE_PAD, ROW_TILE), lambda i: (0, i)),
        ],
        out_shape=[jax.ShapeDtypeStruct((m, d), BF16), jax.ShapeDtypeStruct((GATE_PAD, m), F32)],
        compiler_params=pltpu.CompilerParams(
            dimension_semantics=("parallel",), vmem_limit_bytes=VMEM_LIMIT_BYTES),
        name="prenorm",
    )(x2d, nw.reshape(1, d), w_gate)


def _conv_taps(tail, y, cw):
    rows = y.shape[0]
    ext = jnp.concatenate([tail, y], axis=0)
    acc = cw[CONV_K - 1:CONV_K, :] * y
    for j in range(CONV_K - 1):
        lo = CONV_HALO - (CONV_K - 1) + j
        acc = acc + cw[j:j + 1, :] * ext[lo:lo + rows, :]
    return acc


_ACTIVATIONS = {"none": lambda y: y, "silu": _silu, "sigmoid": _sigmoid}


def _inproj_kernel(h_ref, w_ref, cw_ref, p_ref, wb_ref, tail_ref, *, tile_modes, row_tiles_per_seq):
    j, i = pl.program_id(0), pl.program_id(1)

    @pl.when(i == 0)
    def _():
        for c0 in range(0, COL_TILE, 128):
            wb_ref[:, c0:c0 + 128] = w_ref[c0:c0 + 128, :].T.astype(BF16)

    def matmul():
        return jnp.dot(h_ref[...], wb_ref[...], preferred_element_type=F32)

    for lo, hi, mode in tile_modes:
        @pl.when((j >= lo) & (j < hi))
        def _(mode=mode):
            if mode != "conv_silu":
                p_ref[...] = _ACTIVATIONS[mode](matmul()).astype(p_ref.dtype)
                return

            @pl.when(i % row_tiles_per_seq == 0)
            def _():
                tail_ref[...] = jnp.zeros_like(tail_ref)

            cw = cw_ref[...]
            tail = tail_ref[...]
            y = matmul()
            p_ref[...] = _silu(_conv_taps(tail, y, cw)).astype(p_ref.dtype)
            tail_ref[...] = y[y.shape[0] - CONV_HALO:, :]


def _inproj(h2d, w_in_t, layer, conv_w, tile_modes, seq_len):
    m, d = h2d.shape
    n_main = tile_modes[-1][1] * COL_TILE
    n_conv = conv_w.shape[1] // COL_TILE
    body = functools.partial(_inproj_kernel, tile_modes=tile_modes, row_tiles_per_seq=seq_len // INPROJ_ROW_TILE)
    return pl.pallas_call(
        body,
        grid=(n_main // COL_TILE, m // INPROJ_ROW_TILE),
        in_specs=[
            pl.BlockSpec((INPROJ_ROW_TILE, d), lambda j, i: (i, 0)),
            pl.BlockSpec((None, COL_TILE, d), lambda j, i: (layer, j, 0)),
            pl.BlockSpec((CONV_K, COL_TILE), lambda j, i: (0, jnp.minimum(j, n_conv - 1))),
        ],
        out_specs=pl.BlockSpec((INPROJ_ROW_TILE, COL_TILE), lambda j, i: (i, j)),
        out_shape=jax.ShapeDtypeStruct((m, n_main), BF16),
        scratch_shapes=[pltpu.VMEM((d, COL_TILE), BF16),
                        pltpu.VMEM((CONV_HALO, COL_TILE), F32)],
        compiler_params=pltpu.CompilerParams(
            dimension_semantics=("arbitrary", "arbitrary"), vmem_limit_bytes=VMEM_LIMIT_BYTES),
        name="inproj",
    )(h2d, w_in_t, conv_w)


def _outproj_kernel(g_ref, w_ref, x_ref, nw_ref, wg_ref, xo_ref, h_ref, pg_ref, wb_ref):
    @pl.when(pl.program_id(0) == 0)
    def _():
        wb_ref[...] = w_ref[...].astype(BF16)

    y = x_ref[...] + jnp.dot(g_ref[...], wb_ref[...], preferred_element_type=F32)
    xo_ref[...] = y
    h = _rms(y, nw_ref[...]).astype(BF16)
    h_ref[...] = h
    pg_ref[...] = _gates_t(wg_ref, h)


def _outproj_final_kernel(g_ref, w_ref, x_ref, fw_ref, o_ref, wb_ref):
    @pl.when(pl.program_id(0) == 0)
    def _():
        wb_ref[...] = w_ref[...].astype(BF16)

    y = x_ref[...] + jnp.dot(g_ref[...], wb_ref[...], preferred_element_type=F32)
    o_ref[...] = _rms(y, fw_ref[...])


def _outproj(g2d, w_out, layer, x2d, nw_next, w_gate_next):
    m, d = x2d.shape
    k = g2d.shape[1]
    row = lambda i: (i, 0)
    fixed = lambda i: (0, 0)
    return pl.pallas_call(
        _outproj_kernel,
        grid=(m // OUTPROJ_ROW_TILE,),
        in_specs=[
            pl.BlockSpec((OUTPROJ_ROW_TILE, k), row),
            pl.BlockSpec((None, k, d), lambda i: (layer, 0, 0)),
            pl.BlockSpec((OUTPROJ_ROW_TILE, d), row),
            pl.BlockSpec((1, d), fixed),
            pl.BlockSpec((GATE_PAD, d), fixed),
        ],
        out_specs=[
            pl.BlockSpec((OUTPROJ_ROW_TILE, d), row),
            pl.BlockSpec((OUTPROJ_ROW_TILE, d), row),
            pl.BlockSpec((GATE_PAD, OUTPROJ_ROW_TILE), lambda i: (0, i)),
        ],
        out_shape=[jax.ShapeDtypeStruct((m, d), F32), jax.ShapeDtypeStruct((m, d), BF16),
                   jax.ShapeDtypeStruct((GATE_PAD, m), F32)],
        scratch_shapes=[pltpu.VMEM((k, d), BF16)],
        compiler_params=pltpu.CompilerParams(
            dimension_semantics=("arbitrary",), vmem_limit_bytes=OUTPROJ_VMEM_LIMIT_BYTES),
        name="outproj",
    )(g2d, w_out, x2d, nw_next.reshape(1, d), w_gate_next)


def _outproj_final(g2d, w_out, layer, x2d, final_w):
    m, d = x2d.shape
    k = g2d.shape[1]
    row = lambda i: (i, 0)
    return pl.pallas_call(
        _outproj_final_kernel,
        grid=(m // OUTPROJ_ROW_TILE,),
        in_specs=[
            pl.BlockSpec((OUTPROJ_ROW_TILE, k), row),
            pl.BlockSpec((None, k, d), lambda i: (layer, 0, 0)),
            pl.BlockSpec((OUTPROJ_ROW_TILE, d), row),
            pl.BlockSpec((1, d), lambda i: (0, 0)),
        ],
        out_specs=pl.BlockSpec((OUTPROJ_ROW_TILE, d), row),
        out_shape=jax.ShapeDtypeStruct((m, d), F32),
        scratch_shapes=[pltpu.VMEM((k, d), BF16)],
        compiler_params=pltpu.CompilerParams(
            dimension_semantics=("arbitrary",), vmem_limit_bytes=OUTPROJ_VMEM_LIMIT_BYTES),
        name="outproj_final",
    )(g2d, w_out, x2d, final_w.reshape(1, d))


def _chunk_iotas():
    r = lax.broadcasted_iota(jnp.int32, (CHUNK, CHUNK), 0)
    c = lax.broadcasted_iota(jnp.int32, (CHUNK, CHUNK), 1)
    return r, c


def _gate_spec(heads, last_block, kind, ahead):
    return pl.BlockSpec((None, heads, None, CHUNKS_PER_BLOCK, CHUNK),
                        lambda bi, h, ti: (kind, h, bi, jnp.minimum(ti + ahead, last_block), 0))


def _col(xt, n, width):
    return jnp.broadcast_to(xt[:, n:n + 1], (CHUNK, width))


def _columns(rows_list):
    rows = jnp.concatenate(rows_list, axis=0)
    if rows.shape[0] < 128:
        rows = jnp.concatenate([rows, jnp.zeros((128 - rows.shape[0], CHUNK), F32)], axis=0)
    rows = jnp.concatenate([rows, jnp.zeros((128, 128 - CHUNK), F32)], axis=1)
    return rows.T[:CHUNK, :]


def _unit_lower_inverse_minus_identity(a_list, r, c):
    base = 8
    blk = (r // base) == (c // base)
    a0 = [jnp.where(blk, a, 0.0) for a in a_list]
    a2 = [_dot(x, x) for x in a0]
    a34 = [_dot(jnp.concatenate([x, y], axis=0), y) for x, y in zip(a0, a2)]
    p = [y - x - z[:CHUNK] for x, y, z in zip(a0, a2, a34)]
    a4 = [z[CHUNK:] for z in a34]
    n = [x + y + _dot(x, y) for x, y in zip(p, a4)]
    s = base
    while s < CHUNK:
        off_mask = ((r // (2 * s)) == (c // (2 * s))) & (((r // s) % 2) == 1) & (((c // s) % 2) == 0)
        off = [jnp.where(off_mask, a, 0.0) for a in a_list]
        x = [o + _dot(o, m) for o, m in zip(off, n)]
        n = [m - y - _dot(m, y) for m, y in zip(n, x)]
        s *= 2
    return n


GDN_HEADS_PER_STEP = 8
_GDN_COLUMN_KINDS = 5


def _gdn_kernel(alog_ref, dtb_ref, q_ref, k_ref, v_ref, z_ref, a_ref, bt_ref, nw_ref, o_ref, s_ref):
    hp, nc = GDN_HEADS_PER_STEP, CHUNKS_PER_BLOCK
    hg = pl.program_id(1)

    @pl.when(pl.program_id(2) == 0)
    def _():
        s_ref[...] = jnp.zeros_like(s_ref)

    q = q_ref[0].astype(F32)
    k = k_ref[0].astype(F32)
    v = v_ref[0].astype(F32)

    r, c = _chunk_iotas()
    incl = r >= c
    strict = r > c
    upper = (r <= c).astype(F32)

    qh, kh, vh, gcs, es, rows = [], [], [], [], [], []
    for hd in range(hp):
        h = hg * hp + hd
        qs = q[:, hd * HEAD_QK:(hd + 1) * HEAD_QK]
        ks = k[:, hd * HEAD_QK:(hd + 1) * HEAD_QK]
        qh.append(qs * lax.rsqrt(jnp.sum(qs * qs, axis=-1, keepdims=True) + 1e-6) * (HEAD_QK ** -0.5))
        kh.append(ks * lax.rsqrt(jnp.sum(ks * ks, axis=-1, keepdims=True) + 1e-6))
        vh.append(v[:, hd * HEAD_V:(hd + 1) * HEAD_V])
        neg_rate = -jnp.exp(jnp.full((1, 1), alog_ref[h], F32))
        g = neg_rate * _softplus(a_ref[hd] + dtb_ref[h])
        beta = _sigmoid(bt_ref[hd])
        gc = _dot_f32(g, upper)
        e = jnp.exp(gc)
        gcs.append(gc)
        es.append(e)
        rows += [gc, beta, beta * e, e, jnp.exp(gc[:, CHUNK - 1:CHUNK] - gc)]
    heads_per_transpose = 128 // (_GDN_COLUMN_KINDS * nc)
    cols = [_columns(rows[i * _GDN_COLUMN_KINDS:(i + heads_per_transpose) * _GDN_COLUMN_KINDS])
            for i in range(0, hp, heads_per_transpose)]

    def col(hd, kind, n, width):
        i = ((hd % heads_per_transpose) * _GDN_COLUMN_KINDS + kind) * nc + n
        return _col(cols[hd // heads_per_transpose], i, width)

    chains = [(hd, n) for n in range(nc) for hd in range(hp)]
    q_c = {ch: qh[ch[0]][ch[1] * CHUNK:(ch[1] + 1) * CHUNK] for ch in chains}
    k_c = {ch: kh[ch[0]][ch[1] * CHUNK:(ch[1] + 1) * CHUNK] for ch in chains}
    v_c = {ch: vh[ch[0]][ch[1] * CHUNK:(ch[1] + 1) * CHUNK] for ch in chains}

    kq = {ch: _dot_nt(jnp.concatenate([k_c[ch], q_c[ch]], axis=0), k_c[ch]) for ch in chains}
    a_list, attn = [], {}
    for ch in chains:
        hd, n = ch
        gc_row = gcs[hd][n:n + 1, :]
        decay = jnp.where(incl, jnp.exp(jnp.where(incl, col(hd, 0, n, CHUNK) - gc_row, 0.0)), 0.0)
        a_list.append(jnp.where(strict, col(hd, 1, n, CHUNK) * kq[ch][:CHUNK] * decay, 0.0))
        attn[ch] = (kq[ch][CHUNK:] * decay).astype(BF16)
    n_list = _unit_lower_inverse_minus_identity(a_list, r, c)
    uw, kd = {}, {}
    for ch, nmat in zip(chains, n_list):
        hd, n = ch
        rhs = jnp.concatenate([col(hd, 1, n, HEAD_V) * v_c[ch], col(hd, 2, n, HEAD_QK) * k_c[ch]], axis=1)
        uw[ch] = (rhs + _dot(nmat, rhs)).astype(BF16)
        kd[ch] = (k_c[ch] * col(hd, 4, n, HEAD_QK)).astype(BF16)
    kd_uw = {ch: _dot_tn(kd[ch], uw[ch]) for ch in chains}
    at_uw = {ch: _dot(attn[ch], uw[ch]) for ch in chains}
    lhs = {}
    for ch in chains:
        hd, n = ch
        to_out = q_c[ch] * col(hd, 3, n, HEAD_QK) - at_uw[ch][:, HEAD_V:]
        lhs[ch] = jnp.concatenate([-kd_uw[ch][:, HEAD_V:], to_out], axis=0).astype(BF16)

    nw = nw_ref[...]
    s = [s_ref[hd] for hd in range(hp)]
    for n in range(nc):
        lo = n * CHUNK
        m = [_dot(lhs[(hd, n)], s[hd]) for hd in range(hp)]
        o = [m[hd][HEAD_QK:] + at_uw[(hd, n)][:, :HEAD_V] for hd in range(hp)]
        s = [es[hd][n:n + 1, CHUNK - 1:CHUNK] * s[hd] + m[hd][:HEAD_QK] + kd_uw[(hd, n)][:, :HEAD_V]
             for hd in range(hp)]
        for hd in range(hp):
            z = z_ref[0, lo:lo + CHUNK, hd * HEAD_V:(hd + 1) * HEAD_V]
            o_ref[0, lo:lo + CHUNK, hd * HEAD_V:(hd + 1) * HEAD_V] = (_rms(o[hd], nw) * z.astype(F32)).astype(o_ref.dtype)
    for hd in range(hp):
        s_ref[hd] = s[hd]


def _gdn_recurrence(p3, gates, a_log, dt_bias, norm_w):
    b, t, _ = p3.shape
    tb, hp = TIME_BLOCK, GDN_HEADS_PER_STEP
    wq, wv = hp * HEAD_QK, hp * HEAD_V
    nk = QK_W // wq
    nv = 2 * QK_W // wv
    nz = (2 * QK_W + D_INNER) // wv
    smem = pl.BlockSpec(memory_space=pltpu.SMEM)
    gate_spec = functools.partial(_gate_spec, hp, t // tb - 1)
    return pl.pallas_call(
        _gdn_kernel,
        grid=(b, N_HEADS // hp, t // tb),
        in_specs=[
            smem, smem,
            pl.BlockSpec((1, tb, wq), lambda bi, h, ti: (bi, ti, h)),
            pl.BlockSpec((1, tb, wq), lambda bi, h, ti: (bi, ti, nk + h)),
            pl.BlockSpec((1, tb, wv), lambda bi, h, ti: (bi, ti, nv + h)),
            pl.BlockSpec((1, tb, wv), lambda bi, h, ti: (bi, ti, nz + h)),
            gate_spec(0, 0), gate_spec(1, 0),
            pl.BlockSpec((1, HEAD_V), lambda bi, h, ti: (0, 0)),
        ],
        out_specs=pl.BlockSpec((1, tb, wv), lambda bi, h, ti: (bi, ti, h)),
        out_shape=jax.ShapeDtypeStruct((b, t, D_INNER), BF16),
        scratch_shapes=[pltpu.VMEM((hp, HEAD_QK, HEAD_V), F32)],
        compiler_params=pltpu.CompilerParams(
            dimension_semantics=("parallel", "parallel", "arbitrary"),
            vmem_limit_bytes=VMEM_LIMIT_BYTES),
        name="gdn_recurrence",
    )(a_log, dt_bias, p3, p3, p3, p3, gates, gates, norm_w.reshape(1, HEAD_V))


MLSTM_HEADS_PER_STEP = 2
_MLSTM_COLUMN_KINDS = 4


def _prefix_max_lanes(x):
    y = jnp.concatenate([x, jnp.full((x.shape[0], 128 - CHUNK), -jnp.inf, F32)], axis=1)
    s = 1
    while s < CHUNK:
        y = jnp.maximum(y, pltpu.roll(y, s, axis=1))
        s *= 2
    return y[:, :CHUNK]


MLSTM_CHUNK_GROUPS = 4


def _mlstm_gate_setup(ig_ref, fg_ref, ib_ref, fb_ref, hg, m_ref, cols_ref, d_ref, ws_ref):
    hp, nc = MLSTM_HEADS_PER_STEP, CHUNKS_PER_BLOCK
    r, c = _chunk_iotas()
    upper = (r <= c).astype(F32)
    chunk_id = lax.broadcasted_iota(jnp.int32, (nc, 1), 0)
    rows = []
    for hd in range(hp):
        h = hg * hp + hd
        i_pre = ig_ref[hd] + ib_ref[h]
        lf = -_softplus(-(fg_ref[hd] + fb_ref[h]))
        bc = _dot_f32(lf, upper)
        b_last = bc[:, CHUNK - 1:CHUNK]
        d = i_pre - bc
        g_end = b_last + d
        g_max = jnp.max(g_end, axis=-1, keepdims=True)
        r_max = bc + _prefix_max_lanes(d)
        m = m_ref[hd, 0:1, 0:1]
        m_before = jnp.zeros((nc, 1), F32)
        m_after = jnp.zeros((nc, 1), F32)
        for n in range(nc):
            m_before = jnp.where(chunk_id == n, m, m_before)
            m = jnp.maximum(b_last[n:n + 1, :] + m, g_max[n:n + 1, :])
            m_after = jnp.where(chunk_id == n, m, m_after)
        m_ref[hd] = jnp.broadcast_to(m, m_ref.shape[1:])
        m_t = jnp.maximum(bc + m_before, r_max)
        d_ref[hd] = d
        ws_ref[hd] = jnp.broadcast_to(jnp.exp(b_last + m_before - m_after), ws_ref.shape[1:])
        rows += [bc - m_t, jnp.exp(bc + m_before - m_t), jnp.exp(-m_t), jnp.exp(g_end - m_after)]
    cols_ref[...] = _columns(rows)


def _mlstm_kernel(ib_ref, fb_ref, q_ref, k_ref, v_ref, og_ref, z_ref, ig_ref, fg_ref, ig_next_ref, fg_next_ref,
                  nw_ref, o_ref, s_ref, m_ref, cols_ref, d_ref, ws_ref):
    hp, nc = MLSTM_HEADS_PER_STEP, CHUNKS_PER_BLOCK
    hg = pl.program_id(1)
    setup = functools.partial(_mlstm_gate_setup, ib_ref=ib_ref, fb_ref=fb_ref, hg=hg, m_ref=m_ref,
                              cols_ref=cols_ref, d_ref=d_ref, ws_ref=ws_ref)

    @pl.when(pl.program_id(2) == 0)
    def _():
        s_ref[...] = jnp.zeros_like(s_ref)
        m_ref[...] = jnp.zeros_like(m_ref)
        setup(ig_ref, fg_ref)

    cols = cols_ref[...]
    d_rows = [d_ref[hd] for hd in range(hp)]
    w_state = [ws_ref[hd][:, 0:1] for hd in range(hp)]
    setup(ig_next_ref, fg_next_ref)

    q = q_ref[0].astype(F32)
    k = k_ref[0].astype(F32) * (HEAD_QK ** -0.5)

    r, c = _chunk_iotas()
    incl = r >= c
    ones_col = (lax.broadcasted_iota(jnp.int32, (CHUNK, 128), 1) == 0).astype(BF16)

    def col(hd, kind, n, width):
        return _col(cols, (hd * _MLSTM_COLUMN_KINDS + kind) * nc + n, width)

    nw = nw_ref[...]
    s = [s_ref[hd] for hd in range(hp)]
    for grp in range(MLSTM_CHUNK_GROUPS):
        chunks = range(grp * nc // MLSTM_CHUNK_GROUPS, (grp + 1) * nc // MLSTM_CHUNK_GROUPS)
        chains = [(hd, n) for n in chunks for hd in range(hp)]
        q_c = {ch: q[ch[1] * CHUNK:(ch[1] + 1) * CHUNK, ch[0] * HEAD_QK:(ch[0] + 1) * HEAD_QK] for ch in chains}
        k_c = {ch: k[ch[1] * CHUNK:(ch[1] + 1) * CHUNK, ch[0] * HEAD_QK:(ch[0] + 1) * HEAD_QK] for ch in chains}
        v_ext = {ch: jnp.concatenate([v_ref[0, ch[1] * CHUNK:(ch[1] + 1) * CHUNK, ch[0] * HEAD_V:(ch[0] + 1) * HEAD_V],
                                      ones_col], axis=1) for ch in chains}

        qk = {ch: _dot_nt(q_c[ch], k_c[ch]) for ch in chains}
        amat, kw = {}, {}
        for ch in chains:
            hd, n = ch
            arg = col(hd, 0, n, CHUNK) + d_rows[hd][n:n + 1, :]
            amat[ch] = (qk[ch] * jnp.exp(jnp.where(incl, arg, -jnp.inf))).astype(BF16)
            kw[ch] = (k_c[ch] * col(hd, 3, n, HEAD_QK)).astype(BF16)
        av = {ch: _dot(amat[ch], v_ext[ch]) for ch in chains}
        kv = {ch: _dot_tn(kw[ch], v_ext[ch]) for ch in chains}

        s_in = {}
        for hd in range(hp):
            for n in chunks:
                s_in[(hd, n)] = s[hd]
                s[hd] = w_state[hd][n:n + 1, :] * s[hd] + kv[(hd, n)]
        qs = {ch: _dot(q_c[ch], s_in[ch]) for ch in chains}

        for ch in chains:
            hd, n = ch
            lo = n * CHUNK
            num_ext = col(hd, 1, n, HEAD_V + 128) * qs[ch] + av[ch]
            denom = jnp.maximum(jnp.abs(num_ext[:, HEAD_V:HEAD_V + 1]), col(hd, 2, n, 1))
            hh = num_ext[:, :HEAD_V] / denom
            gate = (og_ref[0, lo:lo + CHUNK, hd * HEAD_V:(hd + 1) * HEAD_V].astype(F32)
                    * z_ref[0, lo:lo + CHUNK, hd * HEAD_V:(hd + 1) * HEAD_V].astype(F32))
            o_ref[0, lo:lo + CHUNK, hd * HEAD_V:(hd + 1) * HEAD_V] = (_rms(hh, nw) * gate).astype(o_ref.dtype)
    for hd in range(hp):
        s_ref[hd] = s[hd]


def _mlstm_recurrence(p3, gates, i_bias, f_bias, norm_w):
    b, t, _ = p3.shape
    tb, hp = TIME_BLOCK, MLSTM_HEADS_PER_STEP
    wq, wv = hp * HEAD_QK, hp * HEAD_V
    nk = QK_W // wq
    nv = 2 * QK_W // wv
    nh = D_INNER // wv
    smem = pl.BlockSpec(memory_space=pltpu.SMEM)
    gate_spec = functools.partial(_gate_spec, hp, t // tb - 1)
    return pl.pallas_call(
        _mlstm_kernel,
        grid=(b, N_HEADS // hp, t // tb),
        in_specs=[
            smem, smem,
            pl.BlockSpec((1, tb, wq), lambda bi, h, ti: (bi, ti, h)),
            pl.BlockSpec((1, tb, wq), lambda bi, h, ti: (bi, ti, nk + h)),
            pl.BlockSpec((1, tb, wv), lambda bi, h, ti: (bi, ti, nv + h)),
            pl.BlockSpec((1, tb, wv), lambda bi, h, ti: (bi, ti, nv + nh + h)),
            pl.BlockSpec((1, tb, wv), lambda bi, h, ti: (bi, ti, nv + 2 * nh + h)),
            gate_spec(0, 0), gate_spec(1, 0), gate_spec(0, 1), gate_spec(1, 1),
            pl.BlockSpec((1, HEAD_V), lambda bi, h, ti: (0, 0)),
        ],
        out_specs=pl.BlockSpec((1, tb, wv), lambda bi, h, ti: (bi, ti, h)),
        out_shape=jax.ShapeDtypeStruct((b, t, D_INNER), BF16),
        scratch_shapes=[
            pltpu.VMEM((hp, HEAD_QK, HEAD_V + 128), F32),
            pltpu.VMEM((hp, 8, 128), F32),
            pltpu.VMEM((CHUNK, 128), F32),
            pltpu.VMEM((hp, CHUNKS_PER_BLOCK, CHUNK), F32),
            pltpu.VMEM((hp, CHUNKS_PER_BLOCK, 128), F32),
        ],
        compiler_params=pltpu.CompilerParams(
            dimension_semantics=("parallel", "parallel", "arbitrary"),
            vmem_limit_bytes=VMEM_LIMIT_BYTES),
        name="mlstm_recurrence",
    )(i_bias, f_bias, p3, p3, p3, p3, p3, gates, gates, gates, gates, norm_w.reshape(1, HEAD_V))


def _tile_modes(*widths_and_modes):
    ranges, lo = [], 0
    for width, mode in widths_and_modes:
        ranges.append((lo, lo + width // COL_TILE, mode))
        lo += width // COL_TILE
    return tuple(ranges)


_GDN_TILE_MODES = _tile_modes((2 * QK_W + D_INNER, "conv_silu"), (D_INNER, "silu"))
_MLSTM_TILE_MODES = _tile_modes((2 * QK_W, "conv_silu"), (D_INNER, "none"), (D_INNER, "sigmoid"),
                                (D_INNER, "silu"))


def _gate_weight_t(w_in_t, layer):
    n_main = w_in_t.shape[1] - 2 * N_HEADS
    return jnp.pad(w_in_t[layer, n_main:, :], ((0, GATE_PAD - 2 * N_HEADS), (0, 0)))


def _gate_rows(pg_t, b, t):
    return pg_t[:2 * N_HEADS].reshape(2, N_HEADS, b, t // CHUNK, CHUNK)


def kernel(x, norm_w, final_norm_w, gdn_w_in, gdn_conv_w, gdn_a_log, gdn_dt_bias, gdn_norm_w, gdn_w_out,
           mlstm_w_in, mlstm_conv_w, mlstm_i_bias, mlstm_f_bias, mlstm_norm_w, mlstm_w_out):
    b, t, d = x.shape
    depth = norm_w.shape[0]
    x2d = x.reshape(b * t, d)
    w_in_t = (jnp.swapaxes(gdn_w_in, 1, 2), jnp.swapaxes(mlstm_w_in, 1, 2))

    h, pg = _prenorm(x2d, norm_w[0], _gate_weight_t(w_in_t[0], 0))
    for i in range(depth):
        j = i // 2
        gates = _gate_rows(pg, b, t)
        if i % 2 == 0:
            p = _inproj(h, w_in_t[0], j, gdn_conv_w[j], _GDN_TILE_MODES, t).reshape(b, t, -1)
            g = _gdn_recurrence(p, gates, gdn_a_log[j], gdn_dt_bias[j], gdn_norm_w[j])
            w_out = gdn_w_out
        else:
            p = _inproj(h, w_in_t[1], j, mlstm_conv_w[j], _MLSTM_TILE_MODES, t).reshape(b, t, -1)
            g = _mlstm_recurrence(p, gates, mlstm_i_bias[j], mlstm_f_bias[j], mlstm_norm_w[j])
            w_out = mlstm_w_out
        g = g.reshape(b * t, D_INNER)
        if i == depth - 1:
            x2d = _outproj_final(g, w_out, j, x2d, final_norm_w)
        else:
            x2d, h, pg = _outproj(g, w_out, j, x2d, norm_w[i + 1], _gate_weight_t(w_in_t[(i + 1) % 2], (i + 1) // 2))
    return x2d.reshape(b, t, d)
```
